```python
import math
import jax, jax.numpy as jnp
from jax import lax
import numpy as np

D_MODEL = 1024
BATCH = 16
SEQ = 2048
DEPTH = 4

GRID_W = 64
CTX_LEN = 256
N_MIXERS = 3
EPS = 1e-6
N_HEADS = 16
QK_NOPE = 64
QK_ROPE = 32
V_DIM = 64
Q_LORA = 256
KV_LORA = 128
ROPE_THETA = 10000.0
Q_BLOCK = 128
CONV_W = 31
SHORT_W = 3
POS_EMB = 33
FILTER_FO = 64
DECAY_FAST = 0.3
DECAY_SLOW = 1.5
DECAY_TARGET = 1e-2
D_FF = 2816
FFN_CONV_W = 3
N_A = (DEPTH + 2) // 3
N_B = (DEPTH + 1) // 3
N_C = DEPTH // 3

kernel_name = 'hybrid_mla_conformer_hyena_dit'

F32 = jnp.float32


def rmsnorm(x, g):
    xf = x.astype(F32)
    y = xf * lax.rsqrt(jnp.mean(xf * xf, axis=-1, keepdims=True) + EPS)
    return (y * g.astype(F32)).astype(x.dtype)


def layernorm(x, g, b):
    xf = x.astype(F32)
    mu = jnp.mean(xf, axis=-1, keepdims=True)
    var = jnp.mean(jnp.square(xf - mu), axis=-1, keepdims=True)
    return ((xf - mu) * lax.rsqrt(var + EPS) * g.astype(F32) + b.astype(F32)).astype(x.dtype)


def modulate(x, g, shift, scale):
    return rmsnorm(x, g) * (1 + scale) + shift


def dwconv(x, w, b):
    pad = (w.shape[0] - 1) // 2
    y = lax.conv_general_dilated(x, w[:, None, :].astype(x.dtype), window_strides=(1,),
                                 padding=[(pad, pad)], dimension_numbers=('NWC', 'WIO', 'NWC'),
                                 feature_group_count=x.shape[-1])
    return y + b


def axial_angles(L):
    rows = L // GRID_W
    row = jnp.repeat(jnp.arange(rows), GRID_W).astype(F32)
    col = jnp.tile(jnp.arange(GRID_W), rows).astype(F32)
    half = QK_ROPE // 2
    inv = ROPE_THETA ** (-(jnp.arange(0, half, 2, dtype=F32) / half))
    return jnp.concatenate([row[:, None] * inv, col[:, None] * inv], axis=-1)


def apply_axial_rope(x, ang):
    q = QK_ROPE // 4
    xs = x.astype(F32).reshape(x.shape[:-1] + (2, 2, q))
    a = ang.reshape(ang.shape[0], 2, q)
    if x.ndim == 4:
        a = a[:, None]
    cos, sin = jnp.cos(a), jnp.sin(a)
    x1, x2 = xs[..., 0, :], xs[..., 1, :]
    out = jnp.stack([x1 * cos - x2 * sin, x1 * sin + x2 * cos], axis=-2)
    return out.reshape(x.shape).astype(x.dtype)


def mla_q(h, w_dq, q_norm, w_uq, qk_gain, ang):
    B, L, _ = h.shape
    cq = rmsnorm(h @ w_dq, q_norm)
    q = (cq @ w_uq).reshape(B, L, N_HEADS, QK_NOPE + QK_ROPE)
    q_nope = rmsnorm(q[..., :QK_NOPE], qk_gain[0, :QK_NOPE])
    q_pe = rmsnorm(q[..., QK_NOPE:], qk_gain[0, QK_NOPE:])
    if ang is not None:
        q_pe = apply_axial_rope(q_pe, ang)
    return q_nope, q_pe


def mla_kv(h, w_dkv, kv_norm, w_ukv, qk_gain, ang):
    B, L, _ = h.shape
    kv = h @ w_dkv
    c_kv = rmsnorm(kv[..., :KV_LORA], kv_norm)
    k_pe = rmsnorm(kv[..., KV_LORA:], qk_gain[1, QK_NOPE:])
    if ang is not None:
        k_pe = apply_axial_rope(k_pe, ang)
    kvu = (c_kv @ w_ukv).reshape(B, L, N_HEADS, QK_NOPE + V_DIM)
    k_nope = rmsnorm(kvu[..., :QK_NOPE], qk_gain[1, :QK_NOPE])
    return k_nope, k_pe, kvu[..., QK_NOPE:]


def mla_attend(q_nope, q_pe, k_nope, k_pe, v):
    scale = (QK_NOPE + QK_ROPE) ** -0.5
    s = jnp.einsum('bqhd,bkhd->bhqk', q_nope, k_nope) + jnp.einsum('bqhr,bkr->bhqk', q_pe, k_pe)
    p = jax.nn.softmax(s.astype(F32) * scale, axis=-1).astype(v.dtype)
    return jnp.einsum('bhqk,bkhd->bqhd', p, v)


def mla_latent_attention(q_nope, q_pe, k_nope, k_pe, v):
    B, L = q_nope.shape[:2]
    nblk = L // Q_BLOCK

    def blocks(t):
        return t.reshape((B, nblk, Q_BLOCK) + t.shape[2:]).swapaxes(0, 1)

    o = lax.map(lambda qb: mla_attend(qb[0], qb[1], k_nope, k_pe, v), (blocks(q_nope), blocks(q_pe)))
    return o.swapaxes(0, 1).reshape(B, L, N_HEADS * V_DIM)


def conformer_conv(h, w1, b1, wdw, bdw, lng, lnb, w2, b2):
    a = h @ w1 + b1
    u = a[..., :D_MODEL] * jax.nn.sigmoid(a[..., D_MODEL:])
    u = dwconv(u, wdw, bdw)
    u = jax.nn.silu(layernorm(u, lng, lnb))
    return u @ w2 + b2


def hyena_filter(L, f_w1, f_b1, f_w2, f_b2, f_w3, sin_freq):
    t = jnp.linspace(0.0, 1.0, L, dtype=F32)[:, None]
    bands = (POS_EMB - 1) // 2
    w = 2.0 * math.pi * jnp.arange(L, dtype=F32) / L
    f = jnp.linspace(1e-4, bands - 1, bands, dtype=F32)
    fw = w[:, None] * f[None, :]
    z = jnp.concatenate([t, jnp.cos(fw), -jnp.sin(fw)], axis=-1)
    fr = sin_freq.astype(F32)
    hdn = jnp.sin(fr * (z @ f_w1.astype(F32) + f_b1.astype(F32)))
    hdn = jnp.sin(fr * (hdn @ f_w2.astype(F32) + f_b2.astype(F32)))
    hf = hdn @ f_w3.astype(F32)
    deltas = jnp.linspace(math.log(DECAY_TARGET) / DECAY_FAST, math.log(DECAY_TARGET) / DECAY_SLOW,
                          D_MODEL, dtype=F32)
    decay = jnp.exp(-t * jnp.abs(deltas))
    h_fwd = hf[:, :D_MODEL] * decay
    h_bwd = hf[:, D_MODEL:] * decay
    k = jnp.concatenate([h_fwd, jnp.zeros((1, D_MODEL), F32), h_bwd[1:][::-1]], axis=0)
    return k / (jnp.sum(jnp.abs(k), axis=0, keepdims=True) + EPS)


def fft_longconv(u, k):
    L = u.shape[1]
    U = jnp.fft.rfft(u.astype(F32), n=2 * L, axis=1)
    K = jnp.fft.rfft(k, n=2 * L, axis=0)
    return jnp.fft.irfft(U * K[None], n=2 * L, axis=1)[:, :L].astype(u.dtype)


def hyena(h, w_in, b_in, w_short, b_short, f_w1, f_b1, f_w2, f_b2, f_w3, sin_freq, skip, w_out, b_out):
    L = h.shape[1]
    u = dwconv(h @ w_in + b_in, w_short, b_short)
    x0, x1, v = jnp.split(u, 3, axis=-1)
    k = hyena_filter(L, f_w1, f_b1, f_w2, f_b2, f_w3, sin_freq)
    v = v * x1
    v = fft_longconv(v, k) + skip * v
    return (v * x0) @ w_out + b_out


def conv_ffn(h, w_up, w_dw, b_dw, w_down):
    a = h @ w_up
    g = dwconv(a[..., :D_FF], w_dw, b_dw)
    return (jax.nn.silu(g) * a[..., D_FF:]) @ w_down


def setup_inputs(seed: int = 0) -> dict:
    key = jax.random.key(seed)
    ks = iter(jax.random.split(key, 41))
    D, H = D_MODEL, N_HEADS

    def nrm(shape, fan_in, s=1.0):
        return jax.random.normal(next(ks), shape, F32) * (s * fan_in ** -0.5)

    def gain(shape):
        return 1.0 + 0.02 * jax.random.normal(next(ks), shape, F32)

    def bias(shape, s=0.02):
        return s * jax.random.normal(next(ks), shape, F32)

    return {
        'x': jax.random.normal(next(ks), (BATCH, SEQ, D), F32),
        'c': jax.random.normal(next(ks), (BATCH, D), F32),
        'ctx': jax.random.normal(next(ks), (BATCH, CTX_LEN, D), F32),
        'c_ctx': jax.random.normal(next(ks), (D,), F32),
        'ada_w': nrm((DEPTH, D, 6 * D), D, 0.5),
        'ada_b': bias((DEPTH, 6 * D)),
        'norm_mix': gain((DEPTH, D)),
        'norm_ffn': gain((DEPTH, D)),
        'mla_w_dq': nrm((N_A, D, Q_LORA), D),
        'mla_q_norm': gain((N_A, Q_LORA)),
        'mla_w_uq': nrm((N_A, Q_LORA, H * (QK_NOPE + QK_ROPE)), Q_LORA),
        'mla_w_dkv': nrm((N_A, D, KV_LORA + QK_ROPE), D),
        'mla_kv_norm': gain((N_A, KV_LORA)),
        'mla_w_ukv': nrm((N_A, KV_LORA, H * (QK_NOPE + V_DIM)), KV_LORA),
        'mla_qk_gain': gain((N_A, 2, QK_NOPE + QK_ROPE)),
        'mla_w_o': nrm((N_A, H * V_DIM, D), H * V_DIM),
        'cf_w_pw1': nrm((N_B, D, 2 * D), D),
        'cf_b_pw1': bias((N_B, 2 * D)),
        'cf_w_dw': nrm((N_B, CONV_W, D), CONV_W),
        'cf_b_dw': bias((N_B, D)),
        'cf_ln_g': gain((N_B, D)),
        'cf_ln_b': bias((N_B, D)),
        'cf_w_pw2': nrm((N_B, D, D), D),
        'cf_b_pw2': bias((N_B, D)),
        'hy_w_in': nrm((N_C, D, 3 * D), D),
        'hy_b_in': bias((N_C, 3 * D)),
        'hy_w_short': nrm((N_C, SHORT_W, 3 * D), SHORT_W),
        'hy_b_short': bias((N_C, 3 * D)),
        'hy_f_w1': nrm((N_C, POS_EMB, FILTER_FO), POS_EMB),
        'hy_f_b1': bias((N_C, FILTER_FO), 0.1),
        'hy_f_w2': nrm((N_C, FILTER_FO, FILTER_FO), FILTER_FO),
        'hy_f_b2': bias((N_C, FILTER_FO), 0.1),
        'hy_f_w3': nrm((N_C, FILTER_FO, 2 * D), FILTER_FO),
        'hy_sin_freq': gain((N_C, FILTER_FO)),
        'hy_skip': bias((N_C, D), 0.5),
        'hy_w_out': nrm((N_C, D, D), D),
        'hy_b_out': bias((N_C, D)),
        'ffn_w_up': nrm((DEPTH, D, 2 * D_FF), D),
        'ffn_w_dw': nrm((DEPTH, FFN_CONV_W, D_FF), FFN_CONV_W),
        'ffn_b_dw': bias((DEPTH, D_FF)),
        'ffn_w_down': nrm((DEPTH, D_FF, D), D_FF),
    }


def reference(x, c, ctx, c_ctx, ada_w, ada_b, norm_mix, norm_ffn,
              mla_w_dq, mla_q_norm, mla_w_uq, mla_w_dkv, mla_kv_norm, mla_w_ukv, mla_qk_gain, mla_w_o,
              cf_w_pw1, cf_b_pw1, cf_w_dw, cf_b_dw, cf_ln_g, cf_ln_b, cf_w_pw2, cf_b_pw2,
              hy_w_in, hy_b_in, hy_w_short, hy_b_short, hy_f_w1, hy_f_b1, hy_f_w2, hy_f_b2, hy_f_w3,
              hy_sin_freq, hy_skip, hy_w_out, hy_b_out,
              ffn_w_up, ffn_w_dw, ffn_b_dw, ffn_w_down):
    B, L, _ = x.shape
    Lc = ctx.shape[1]
    ang = axial_angles(L)
    s_lat = jax.nn.silu(c)[:, None, :]
    s_ctx = jax.nn.silu(c_ctx)
    for i in range(DEPTH):
        kind = i % N_MIXERS
        j = i // N_MIXERS
        need_ctx_out = i < DEPTH - 1
        sh1, sc1, g1, sh2, sc2, g2 = jnp.split(s_lat @ ada_w[i] + ada_b[i], 6, axis=-1)
        csh1, csc1, cg1, csh2, csc2, cg2 = jnp.split(s_ctx @ ada_w[i] + ada_b[i], 6, axis=-1)
        hl = modulate(x, norm_mix[i], sh1, sc1)
        yc = None
        if kind == 0:
            hc = modulate(ctx, norm_mix[i], csh1, csc1)
            q_nope, q_pe = mla_q(hl, mla_w_dq[j], mla_q_norm[j], mla_w_uq[j], mla_qk_gain[j], ang)
            kl = mla_kv(hl, mla_w_dkv[j], mla_kv_norm[j], mla_w_ukv[j], mla_qk_gain[j], ang)
            kc = mla_kv(hc, mla_w_dkv[j], mla_kv_norm[j], mla_w_ukv[j], mla_qk_gain[j], None)
            k_nope = jnp.concatenate([kc[0], kl[0]], axis=1)
            k_pe = jnp.concatenate([kc[1], kl[1]], axis=1)
            v = jnp.concatenate([kc[2], kl[2]], axis=1)
            yl = mla_latent_attention(q_nope, q_pe, k_nope, k_pe, v) @ mla_w_o[j]
            if need_ctx_out:
                qc_nope, qc_pe = mla_q(hc, mla_w_dq[j], mla_q_norm[j], mla_w_uq[j], mla_qk_gain[j], None)
                yc = mla_attend(qc_nope, qc_pe, *kc).reshape(B, Lc, N_HEADS * V_DIM) @ mla_w_o[j]
        elif kind == 1:
            cf = (cf_w_pw1[j], cf_b_pw1[j], cf_w_dw[j], cf_b_dw[j], cf_ln_g[j], cf_ln_b[j], cf_w_pw2[j], cf_b_pw2[j])
            yl = conformer_conv(hl, *cf)
            if need_ctx_out:
                yc = conformer_conv(modulate(ctx, norm_mix[i], csh1, csc1), *cf)
        else:
            hy = (hy_w_in[j], hy_b_in[j], hy_w_short[j], hy_b_short[j], hy_f_w1[j], hy_f_b1[j], hy_f_w2[j],
                  hy_f_b2[j], hy_f_w3[j], hy_sin_freq[j], hy_skip[j], hy_w_out[j], hy_b_out[j])
            yl = hyena(hl, *hy)
            if need_ctx_out:
                yc = hyena(modulate(ctx, norm_mix[i], csh1, csc1), *hy)
        x = x + g1 * yl
        ffn = (ffn_w_up[i], ffn_w_dw[i], ffn_b_dw[i], ffn_w_down[i])
        x = x + g2 * conv_ffn(modulate(x, norm_ffn[i], sh2, sc2), *ffn)
        if need_ctx_out:
            ctx = ctx + cg1 * yc
            ctx = ctx + cg2 * conv_ffn(modulate(ctx, norm_ffn[i], csh2, csc2), *ffn)
    return x
```

```python
import functools
import math

import jax
import jax.numpy as jnp
from jax import lax
from jax.experimental import pallas as pl
from jax.experimental.pallas import tpu as pltpu

F32 = jnp.float32
BF16 = jnp.bfloat16

D_MODEL = 1024
DEPTH = 4
GRID_W = 64
N_MIXERS = 3
EPS = 1e-6
N_HEADS = 16
QK_NOPE = 64
QK_ROPE = 32
V_DIM = 64
Q_LORA = 256
KV_LORA = 128
ROPE_THETA = 10000.0
CONV_W = 31
POS_EMB = 33
FILTER_FO = 64
DECAY_FAST = 0.3
DECAY_SLOW = 1.5
DECAY_TARGET = 1e-2
D_FF = 2816

LANES = 128
HEAD_SLOT = 128
MIB = 1024 * 1024
VMEM_BIG = 58 * MIB
VMEM_MID = 48 * MIB


def _cparams(sem, vmem=VMEM_MID):
    return pltpu.CompilerParams(dimension_semantics=sem, vmem_limit_bytes=vmem)


def _sigmoid(x):
    return 1.0 / (1.0 + jnp.exp(-x))


def _silu(x):
    return x * _sigmoid(x)


def _rms_scale(x, n):
    return lax.rsqrt(jnp.sum(x * x, axis=-1, keepdims=True) * (1.0 / n) + EPS)


def _modulate(x, gain, shift, scale):
    y = x * _rms_scale(x, x.shape[-1]) * gain
    return y * (1.0 + scale) + shift


def _dot(a, b):
    return jnp.dot(a, b, preferred_element_type=F32)


def _ada_kernel(c_ref, w_ref, b_ref, o_ref):
    s = _silu(c_ref[...]).astype(BF16)
    o_ref[0] = _dot(s, w_ref[0].astype(BF16)) + b_ref[0]


def _ada_all(cvec, ada_w, ada_b):
    rows = cvec.shape[0]
    depth, d, n = ada_w.shape
    tn = 1536
    return pl.pallas_call(
        _ada_kernel,
        out_shape=jax.ShapeDtypeStruct((depth, rows, n), F32),
        grid=(depth, n // tn),
        in_specs=[
            pl.BlockSpec((rows, d), lambda i, j: (0, 0)),
            pl.BlockSpec((1, d, tn), lambda i, j: (i, 0, j)),
            pl.BlockSpec((1, 1, tn), lambda i, j: (i, 0, j)),
        ],
        out_specs=pl.BlockSpec((1, rows, tn), lambda i, j: (i, 0, j)),
        compiler_params=_cparams(("parallel", "parallel")),
        name="ada_mod",
    )(cvec, ada_w, ada_b.reshape(depth, 1, n))


def _proj_resid_kernel(a_ref, w_ref, b_ref, x_ref, mod_ref, o_ref, *, gate_row):
    y = _dot(a_ref[0], w_ref[...]) + b_ref[...]
    o_ref[0] = x_ref[0] + mod_ref[0, gate_row:gate_row + 1, :] * y


def _proj_resid(a, w, bias, x, mod, gate_row, tl):
    B, L, K = a.shape
    D = w.shape[1]
    tl = min(tl, L)
    mod_b = mod.shape[0] > 1
    return pl.pallas_call(
        functools.partial(_proj_resid_kernel, gate_row=gate_row),
        out_shape=jax.ShapeDtypeStruct((B, L, D), F32),
        grid=(B, L // tl),
        in_specs=[
            pl.BlockSpec((1, tl, K), lambda b, t: (b, t, 0)),
            pl.BlockSpec((K, D), lambda b, t: (0, 0)),
            pl.BlockSpec((1, D), lambda b, t: (0, 0)),
            pl.BlockSpec((1, tl, D), lambda b, t: (b, t, 0)),
            pl.BlockSpec((1, 6, D), (lambda b, t: (b, 0, 0)) if mod_b else (lambda b, t: (0, 0, 0))),
        ],
        out_specs=pl.BlockSpec((1, tl, D), lambda b, t: (b, t, 0)),
        compiler_params=_cparams(("parallel", "parallel")),
        name="proj_resid",
    )(a, w, bias, x, mod)


FFN_TILE = 256
HALO = 8


def _ffn_kernel(x_ref, mod_ref, gain_ref, wup_ref, wdw_ref, bdw_ref, wdn_ref, o_ref, h_ref, pad_ref,
                *, L, tf):
    j = pl.program_id(1)
    last = pl.num_programs(1) - 1

    @pl.when(j == 0)
    def _():
        m = mod_ref[0]
        h = _modulate(x_ref[0], gain_ref[...], m[3:4], m[4:5])
        h_ref[...] = h.astype(BF16)
        pad_ref[0:HALO, :] = jnp.zeros((HALO, tf), F32)
        pad_ref[HALO + L:2 * HALO + L, :] = jnp.zeros((HALO, tf), F32)

    a = _dot(h_ref[...], wup_ref[0])
    pad_ref[HALO:HALO + L, :] = a[:, :tf]
    w = wdw_ref[0]
    g = (pad_ref[HALO - 1:HALO - 1 + L, :] * w[0:1] + pad_ref[HALO:HALO + L, :] * w[1:2]
         + pad_ref[HALO + 1:HALO + 1 + L, :] * w[2:3] + bdw_ref[0])
    act = (_silu(g) * a[:, tf:]).astype(BF16)
    y = _dot(act, wdn_ref[0])

    @pl.when(j == 0)
    def _():
        o_ref[0] = y

    @pl.when(j > 0)
    def _():
        o_ref[0] += y

    @pl.when(j == last)
    def _():
        o_ref[0] = x_ref[0] + mod_ref[0, 5:6, :] * o_ref[0]


def _prep_ffn(w_up, w_dw, b_dw, w_down):
    tf = FFN_TILE
    D = w_up.shape[0]
    nj = D_FF // tf
    wup = w_up.reshape(D, 2, nj, tf).transpose(2, 0, 1, 3).reshape(nj, D, 2 * tf).astype(BF16)
    wdw = w_dw.reshape(3, nj, tf).transpose(1, 0, 2)
    bdw = b_dw.reshape(nj, 1, tf)
    wdn = w_down.reshape(nj, tf, D).astype(BF16)
    return wup, wdw, bdw, wdn


def _ffn(x, mod, gain, ffn_w):
    wup, wdw, bdw, wdn = ffn_w
    B, L, D = x.shape
    tf = FFN_TILE
    nj = D_FF // tf
    mod_b = mod.shape[0] > 1
    return pl.pallas_call(
        functools.partial(_ffn_kernel, L=L, tf=tf),
        out_shape=jax.ShapeDtypeStruct((B, L, D), F32),
        grid=(B, nj),
        in_specs=[
            pl.BlockSpec((1, L, D), lambda b, j: (b, 0, 0)),
            pl.BlockSpec((1, 6, D), (lambda b, j: (b, 0, 0)) if mod_b else (lambda b, j: (0, 0, 0))),
            pl.BlockSpec((1, D), lambda b, j: (0, 0)),
            pl.BlockSpec((1, D, 2 * tf), lambda b, j: (j, 0, 0)),
            pl.BlockSpec((1, 3, tf), lambda b, j: (j, 0, 0)),
            pl.BlockSpec((1, 1, tf), lambda b, j: (j, 0, 0)),
            pl.BlockSpec((1, tf, D), lambda b, j: (j, 0, 0)),
        ],
        out_specs=pl.BlockSpec((1, L, D), lambda b, j: (b, 0, 0)),
        scratch_shapes=[pltpu.VMEM((L, D), BF16), pltpu.VMEM((L + 2 * HALO, tf), F32)],
        compiler_params=_cparams(("parallel", "arbitrary"), VMEM_BIG),
        name="conv_ffn",
    )(x, mod, gain, wup, wdw, bdw, wdn)


MLA_TL = 256


def _rope_partner(j):
    return j + 8 if (j % 16) < 8 else j - 8


def _rope_table(n_ctx, L):
    rows = L // GRID_W
    row = jnp.repeat(jnp.arange(rows), GRID_W).astype(F32)
    col = jnp.tile(jnp.arange(GRID_W), rows).astype(F32)
    half = QK_ROPE // 2
    inv = ROPE_THETA ** (-(jnp.arange(0, half, 2, dtype=F32) / half))
    ang = jnp.concatenate([row[:, None] * inv, col[:, None] * inv], axis=-1)
    ang = jnp.concatenate([ang, jnp.zeros((n_ctx, QK_ROPE // 2), F32)], axis=0)
    idx = jnp.array([(j // 16) * 8 + (j % 8) for j in range(QK_ROPE)])
    sgn = jnp.array([-1.0 if (j % 16) < 8 else 1.0 for j in range(QK_ROPE)], F32)
    a = ang[:, idx]
    ones = jnp.ones((ang.shape[0], QK_NOPE), F32)
    return jnp.concatenate([ones, jnp.cos(a), jnp.sin(a) * sgn], axis=-1)


def _prep_mla(w_dq, q_norm, w_uq, w_dkv, kv_norm, w_ukv, qk_gain, w_o):
    H = N_HEADS
    perm = jnp.array([_rope_partner(j) for j in range(QK_ROPE)])
    D = w_dq.shape[0]
    w_kpe = w_dkv[:, KV_LORA:]
    wd = jnp.concatenate([w_dq, w_dkv[:, :KV_LORA], jnp.zeros((D, QK_NOPE), F32), w_kpe, w_kpe[:, perm]],
                         axis=1).astype(BF16)
    wq = w_uq.reshape(Q_LORA, H, QK_NOPE + QK_ROPE)
    wq_pe = wq[:, :, QK_NOPE:]
    wuq = jnp.concatenate([wq, wq_pe[:, :, perm]], axis=-1).reshape(Q_LORA, H * HEAD_SLOT).astype(BF16)
    wkv = w_ukv.reshape(KV_LORA, H, QK_NOPE + V_DIM)
    wk = jnp.concatenate([wkv[:, :, :QK_NOPE], jnp.zeros((KV_LORA, H, HEAD_SLOT - QK_NOPE), F32)], axis=-1)
    wukv = jnp.concatenate([wk.reshape(KV_LORA, H * HEAD_SLOT), wkv[:, :, QK_NOPE:].reshape(KV_LORA, H * V_DIM)],
                           axis=1).astype(BF16)
    scale = (QK_NOPE + QK_ROPE) ** -0.5
    gq = qk_gain[0]
    gq_slot = jnp.concatenate([gq, gq[QK_NOPE:][perm]]) * scale
    gk = qk_gain[1]
    gk_pe_slot = jnp.concatenate([jnp.zeros((QK_NOPE,), F32), gk[QK_NOPE:], gk[QK_NOPE:][perm]])
    gk_n_slot = jnp.concatenate([gk[:QK_NOPE], jnp.zeros((HEAD_SLOT - QK_NOPE,), F32)])
    gains = jnp.stack([gq_slot, gk_pe_slot, gk_n_slot], axis=0)
    return (wd, q_norm.reshape(1, Q_LORA), kv_norm.reshape(1, KV_LORA), wuq, wukv, gains, w_o.astype(BF16))


def _mla_prep_kernel(ctx_ref, x_ref, modl_ref, modc_ref, gain_ref, wd_ref, qn_ref, kvn_ref, wuq_ref, wukv_ref,
                     gains_ref, rope_ref, q_ref, k_ref, v_ref):
    is_ctx = pl.program_id(1) == pl.num_programs(1) - 1
    xin = jnp.where(is_ctx, ctx_ref[0], x_ref[0])
    m = jnp.where(is_ctx, modc_ref[0], modl_ref[0])
    h = _modulate(xin, gain_ref[...], m[0:1], m[1:2]).astype(BF16)
    d = _dot(h, wd_ref[...])
    cq = d[:, :Q_LORA]
    cq = (cq * _rms_scale(cq, Q_LORA) * qn_ref[...]).astype(BF16)
    ckv = d[:, Q_LORA:Q_LORA + KV_LORA]
    ckv = (ckv * _rms_scale(ckv, KV_LORA) * kvn_ref[...]).astype(BF16)

    lane = lax.broadcasted_iota(jnp.int32, (1, HEAD_SLOT), 1)
    nope_mask = lane < QK_NOPE
    pe_mask = jnp.logical_and(lane >= QK_NOPE, lane < QK_NOPE + QK_ROPE)
    rope = rope_ref[...]
    g_q = gains_ref[0:1, :]
    g_kpe = gains_ref[1:2, :]
    g_kn = gains_ref[2:3, :]

    def rotate_slot(t):
        return jnp.where(pe_mask, t + pltpu.roll(t, HEAD_SLOT - QK_ROPE, 1), 0.0)

    kp = d[:, Q_LORA + KV_LORA:]
    r_kp = _rms_scale(jnp.where(pe_mask, kp, 0.0), QK_ROPE)
    kpe_slot = rotate_slot(kp * r_kp * g_kpe * rope)

    qraw = _dot(cq, wuq_ref[...])
    kvu = _dot(ckv, wukv_ref[...])
    for hd in range(N_HEADS):
        sl = slice(hd * HEAD_SLOT, (hd + 1) * HEAD_SLOT)
        r = qraw[:, sl]
        rr = jnp.where(nope_mask, _rms_scale(jnp.where(nope_mask, r, 0.0), QK_NOPE),
                       _rms_scale(jnp.where(pe_mask, r, 0.0), QK_ROPE))
        t = r * rr * g_q * rope
        qh = jnp.where(nope_mask, t, rotate_slot(t))
        q_ref[0, :, sl] = qh.astype(BF16)
        kn = kvu[:, sl]
        kh = jnp.where(nope_mask, kn * _rms_scale(kn, QK_NOPE) * g_kn, kpe_slot)
        k_ref[0, :, sl] = kh.astype(BF16)
    v_ref[0] = kvu[:, N_HEADS * HEAD_SLOT:].astype(BF16)


def _mla_prep(ctx, x, modl, modc, gain, mw, rope):
    wd, qn, kvn, wuq, wukv, gains, _ = mw
    B, L, D = x.shape
    Lc = ctx.shape[1]
    tl = MLA_TL
    assert Lc == tl and L % tl == 0
    nt = (Lc + L) // tl
    Lt = Lc + L
    H = N_HEADS
    const = lambda b, t: (0, 0)
    return pl.pallas_call(
        _mla_prep_kernel,
        out_shape=(jax.ShapeDtypeStruct((B, Lt, H * HEAD_SLOT), BF16),
                   jax.ShapeDtypeStruct((B, Lt, H * HEAD_SLOT), BF16),
                   jax.ShapeDtypeStruct((B, Lt, H * V_DIM), BF16)),
        grid=(B, nt),
        in_specs=[
            pl.BlockSpec((1, tl, D), lambda b, t: (b, 0, 0)),
            pl.BlockSpec((1, tl, D), lambda b, t: (b, jnp.minimum(t, nt - 2), 0)),
            pl.BlockSpec((1, 6, D), lambda b, t: (b, 0, 0)),
            pl.BlockSpec((1, 6, D), lambda b, t: (0, 0, 0)),
            pl.BlockSpec((1, D), const),
            pl.BlockSpec(wd.shape, const),
            pl.BlockSpec(qn.shape, const),
            pl.BlockSpec(kvn.shape, const),
            pl.BlockSpec(wuq.shape, const),
            pl.BlockSpec(wukv.shape, const),
            pl.BlockSpec(gains.shape, const),
            pl.BlockSpec((tl, HEAD_SLOT), lambda b, t: (t, 0)),
        ],
        out_specs=(pl.BlockSpec((1, tl, H * HEAD_SLOT), lambda b, t: (b, t, 0)),
                   pl.BlockSpec((1, tl, H * HEAD_SLOT), lambda b, t: (b, t, 0)),
                   pl.BlockSpec((1, tl, H * V_DIM), lambda b, t: (b, t, 0))),
        compiler_params=_cparams(("parallel", "parallel")),
        name="mla_prep",
    )(ctx, x, modl, modc, gain, wd, qn, kvn, wuq, wukv, gains, rope)


HEADS_PER_STEP = 2
ATTN_TQ = 512


def _attn_kernel(q_ref, k_ref, v_ref, o_ref):
    lane = lax.broadcasted_iota(jnp.int32, (1, LANES), 1)
    v = v_ref[0]
    outs = []
    for hh in range(HEADS_PER_STEP):
        sl = slice(hh * HEAD_SLOT, (hh + 1) * HEAD_SLOT)
        s = lax.dot_general(q_ref[0, :, sl], k_ref[0, :, sl], (((1,), (1,)), ((), ())),
                            preferred_element_type=F32)
        p = jnp.exp(s - jnp.max(s, axis=-1, keepdims=True))
        l = jnp.sum(p, axis=-1, keepdims=True)
        outs.append(_dot(p.astype(BF16), v) / l)
    o_ref[0] = jnp.where(lane < V_DIM, outs[0], outs[1]).astype(o_ref.dtype)


def _attention(q, k, v, q_tile0, n_q, tq, k_tile0, n_k):
    B = q.shape[0]
    H = N_HEADS
    nhp = H // HEADS_PER_STEP
    wq = HEADS_PER_STEP * HEAD_SLOT
    wv = HEADS_PER_STEP * V_DIM
    return pl.pallas_call(
        _attn_kernel,
        out_shape=jax.ShapeDtypeStruct((B, n_q, H * V_DIM), BF16),
        grid=(B, nhp, n_q // tq),
        in_specs=[
            pl.BlockSpec((1, tq, wq), lambda b, h, t: (b, q_tile0 + t, h)),
            pl.BlockSpec((1, n_k, wq), lambda b, h, t: (b, k_tile0, h)),
            pl.BlockSpec((1, n_k, wv), lambda b, h, t: (b, k_tile0, h)),
        ],
        out_specs=pl.BlockSpec((1, tq, wv), lambda b, h, t: (b, t, h)),
        compiler_params=_cparams(("parallel", "parallel", "arbitrary")),
        name="mla_attn",
    )(q, k, v)


CH_TILE = 256
CONV_PAD = 16
CONV_ROWS = 128


def _conf_a_kernel(x_ref, mod_ref, gain_ref, w1_ref, b1_ref, wdw_ref, bdw_ref, o_ref, h_ref, pad_ref, *, L, tc):
    c = pl.program_id(1)

    @pl.when(c == 0)
    def _():
        m = mod_ref[0]
        h_ref[...] = _modulate(x_ref[0], gain_ref[...], m[0:1], m[1:2]).astype(BF16)
        pad_ref[0:CONV_PAD, :] = jnp.zeros((CONV_PAD, tc), F32)
        pad_ref[CONV_PAD + L:2 * CONV_PAD + L, :] = jnp.zeros((CONV_PAD, tc), F32)

    a = _dot(h_ref[...], w1_ref[0]) + b1_ref[0]
    pad_ref[CONV_PAD:CONV_PAD + L, :] = a[:, :tc] * _sigmoid(a[:, tc:])
    w = wdw_ref[...]
    bias = bdw_ref[...]
    half = (CONV_W - 1) // 2
    rows = min(CONV_ROWS, L)
    for r0 in range(0, L, rows):
        acc = jnp.broadcast_to(bias, (rows, tc))
        for k in range(CONV_W):
            start = CONV_PAD - half + k + r0
            acc = acc + pad_ref[start:start + rows, :] * w[k:k + 1]
        o_ref[0, r0:r0 + rows, :] = acc


def _conf_b_kernel(u_ref, lng_ref, lnb_ref, w2_ref, b2_ref, x_ref, mod_ref, o_ref):
    u = u_ref[0]
    mu = jnp.mean(u, axis=-1, keepdims=True)
    uc = u - mu
    var = jnp.mean(uc * uc, axis=-1, keepdims=True)
    y = uc * lax.rsqrt(var + EPS) * lng_ref[...] + lnb_ref[...]
    y = _dot(_silu(y).astype(BF16), w2_ref[...]) + b2_ref[...]
    o_ref[0] = x_ref[0] + mod_ref[0, 2:3, :] * y


def _prep_conf(w1, b1, wdw, bdw, lng, lnb, w2, b2):
    D = w1.shape[0]
    tc = CH_TILE
    nc = D // tc
    w1r = w1.reshape(D, 2, nc, tc).transpose(2, 0, 1, 3).reshape(nc, D, 2 * tc).astype(BF16)
    b1r = b1.reshape(2, nc, tc).transpose(1, 0, 2).reshape(nc, 1, 2 * tc)
    return (w1r, b1r, wdw, bdw.reshape(1, D), lng.reshape(1, D), lnb.reshape(1, D), w2.astype(BF16),
            b2.reshape(1, D))


def _conformer(x, mod, gain, cw):
    w1r, b1r, wdw, bdw, lng, lnb, w2, b2 = cw
    B, L, D = x.shape
    tc = CH_TILE
    nc = D // tc
    mod_b = mod.shape[0] > 1
    mod_map2 = (lambda b, c: (b, 0, 0)) if mod_b else (lambda b, c: (0, 0, 0))
    u = pl.pallas_call(
        functools.partial(_conf_a_kernel, L=L, tc=tc),
        out_shape=jax.ShapeDtypeStruct((B, L, D), F32),
        grid=(B, nc),
        in_specs=[
            pl.BlockSpec((1, L, D), lambda b, c: (b, 0, 0)),
            pl.BlockSpec((1, 6, D), mod_map2),
            pl.BlockSpec((1, D), lambda b, c: (0, 0)),
            pl.BlockSpec((1, D, 2 * tc), lambda b, c: (c, 0, 0)),
            pl.BlockSpec((1, 1, 2 * tc), lambda b, c: (c, 0, 0)),
            pl.BlockSpec((CONV_W, tc), lambda b, c: (0, c)),
            pl.BlockSpec((1, tc), lambda b, c: (0, c)),
        ],
        out_specs=pl.BlockSpec((1, L, tc), lambda b, c: (b, 0, c)),
        scratch_shapes=[pltpu.VMEM((L, D), BF16), pltpu.VMEM((L + 2 * CONV_PAD, tc), F32)],
        compiler_params=_cparams(("parallel", "arbitrary")),
        name="conformer_glu_dwconv",
    )(x, mod, gain, w1r, b1r, wdw, bdw)
    tl = min(512, L)
    const = lambda b, t: (0, 0)
    return pl.pallas_call(
        _conf_b_kernel,
        out_shape=jax.ShapeDtypeStruct((B, L, D), F32),
        grid=(B, L // tl),
        in_specs=[
            pl.BlockSpec((1, tl, D), lambda b, t: (b, t, 0)),
            pl.BlockSpec((1, D), const),
            pl.BlockSpec((1, D), const),
            pl.BlockSpec((D, D), const),
            pl.BlockSpec((1, D), const),
            pl.BlockSpec((1, tl, D), lambda b, t: (b, t, 0)),
            pl.BlockSpec((1, 6, D), (lambda b, t: (b, 0, 0)) if mod_b else (lambda b, t: (0, 0, 0))),
        ],
        out_specs=pl.BlockSpec((1, tl, D), lambda b, t: (b, t, 0)),
        compiler_params=_cparams(("parallel", "parallel")),
        name="conformer_ln_pw2",
    )(u, lng, lnb, w2, b2, x, mod)


def _dft_tables(L):
    idx = jnp.arange(L, dtype=jnp.int32)
    prod = (idx[:, None] * idx[None, :]) % (2 * L)
    ang = prod.astype(F32) * (math.pi / L)
    return jnp.cos(ang).astype(BF16), jnp.sin(ang).astype(BF16)


def _filter_features(L):
    t = jnp.linspace(0.0, 1.0, L, dtype=F32)[:, None]
    bands = (POS_EMB - 1) // 2
    w = 2.0 * math.pi * jnp.arange(L, dtype=F32) / L
    f = jnp.linspace(1e-4, bands - 1, bands, dtype=F32)
    fw = w[:, None] * f[None, :]
    z = jnp.concatenate([t, jnp.cos(fw), -jnp.sin(fw)], axis=-1)
    z = jnp.concatenate([z, jnp.zeros((L, LANES - POS_EMB), F32)], axis=-1)
    return t, z.astype(BF16)


def _hy_filter_kernel(z_ref, t_ref, dl_ref, w1_ref, b1_ref, w2_ref, b2_ref, w3f_ref, w3b_ref, fr_ref,
                      cos_ref, sin_ref, kc_ref, ks_ref, kny_ref, *, L):
    fr = fr_ref[...]
    hdn = jnp.sin(fr * (_dot(z_ref[...], w1_ref[...]) + b1_ref[...]))
    hdn = jnp.sin(fr * (_dot(hdn.astype(BF16), w2_ref[...]) + b2_ref[...])).astype(BF16)
    decay = jnp.exp(-t_ref[...] * jnp.abs(dl_ref[...]))
    row = lax.broadcasted_iota(jnp.int32, (L, 1), 0)
    h_fwd = _dot(hdn, w3f_ref[...]) * decay
    h_bwd = jnp.where(row > 0, _dot(hdn, w3b_ref[...]) * decay, 0.0)
    nrm = (jnp.sum(jnp.abs(h_fwd), axis=0, keepdims=True) + jnp.sum(jnp.abs(h_bwd), axis=0, keepdims=True) + EPS)
    inv = 1.0 / nrm
    ksum = (h_fwd + h_bwd) * inv
    kdif = (h_fwd - h_bwd) * inv
    n = 2 * L
    wcol = jnp.where(row == 0, 1.0 / n, 2.0 / n)
    sgn = jnp.where(jnp.bitwise_and(row, 1) == 0, 1.0, -1.0)

    def split_dot(tab, kk):
        hi = kk.astype(BF16)
        lo = (kk - hi.astype(F32)).astype(BF16)
        return _dot(tab, hi) + _dot(tab, lo)

    kc_ref[...] = split_dot(cos_ref[...], ksum) * wcol
    ks_ref[...] = -split_dot(sin_ref[...], kdif) * wcol
    kny_ref[...] = jnp.sum(sgn * ksum, axis=0, keepdims=True) * (1.0 / n)


def _hy_in_kernel(x_ref, mod_ref, gain_ref, win_ref, bin_ref, wsh_ref, bsh_ref, x0_ref, vx_ref, h_ref, pad_ref,
                  *, L, tc):
    c = pl.program_id(1)

    @pl.when(c == 0)
    def _():
        m = mod_ref[0]
        h_ref[...] = _modulate(x_ref[0], gain_ref[...], m[0:1], m[1:2]).astype(BF16)
        pad_ref[0:HALO, :] = jnp.zeros((HALO, 3 * tc), F32)
        pad_ref[HALO + L:2 * HALO + L, :] = jnp.zeros((HALO, 3 * tc), F32)

    pad_ref[HALO:HALO + L, :] = _dot(h_ref[...], win_ref[0]) + bin_ref[0]
    w = wsh_ref[0]
    u = (pad_ref[HALO - 1:HALO - 1 + L, :] * w[0:1] + pad_ref[HALO:HALO + L, :] * w[1:2]
         + pad_ref[HALO + 1:HALO + 1 + L, :] * w[2:3] + bsh_ref[0])
    x0_ref[0] = u[:, :tc]
    vx_ref[0] = u[:, 2 * tc:] * u[:, tc:2 * tc]


def _hy_conv_kernel(vx_ref, x0_ref, cr_ref, sr_ref, cc_ref, sc_ref, kc_ref, ks_ref, kny_ref, skip_ref, o_ref,
                    u_ref, acc_ref, *, L):
    f = pl.program_id(2)
    last = pl.num_programs(2) - 1

    @pl.when(f == 0)
    def _():
        u_ref[...] = vx_ref[0].astype(BF16)

    u = u_ref[...]
    pc = _dot(cr_ref[...], u)
    ps = _dot(sr_ref[...], u)
    kc = kc_ref[...]
    ks = ks_ref[...]
    zc = (pc * kc + ps * ks).astype(BF16)
    zs = (ps * kc - pc * ks).astype(BF16)
    y = _dot(cc_ref[...], zc) + _dot(sc_ref[...], zs)

    @pl.when(f == 0)
    def _():
        acc_ref[...] = y

    @pl.when(f > 0)
    def _():
        acc_ref[...] += y

    @pl.when(f == last)
    def _():
        vx = vx_ref[0]
        row = lax.broadcasted_iota(jnp.int32, (L, 1), 0)
        sgn = jnp.where(jnp.bitwise_and(row, 1) == 0, 1.0, -1.0)
        u_ny = jnp.sum(sgn * vx, axis=0, keepdims=True)
        yy = acc_ref[...] + sgn * (u_ny * kny_ref[...])
        o_ref[0] = ((yy + skip_ref[...] * vx) * x0_ref[0]).astype(o_ref.dtype)


def _prep_hyena(w_in, b_in, w_short, b_short, f_w1, f_b1, f_w2, f_b2, f_w3, sin_freq, skip, w_out, b_out):
    D = w_in.shape[0]
    tc = CH_TILE
    nc = D // tc
    winr = w_in.reshape(D, 3, nc, tc).transpose(2, 0, 1, 3).reshape(nc, D, 3 * tc).astype(BF16)
    binr = b_in.reshape(3, nc, tc).transpose(1, 0, 2).reshape(nc, 1, 3 * tc)
    wshr = w_short.reshape(3, 3, nc, tc).transpose(2, 0, 1, 3).reshape(nc, 3, 3 * tc)
    bshr = b_short.reshape(3, nc, tc).transpose(1, 0, 2).reshape(nc, 1, 3 * tc)
    fw1 = jnp.concatenate([f_w1, jnp.zeros((LANES - POS_EMB, FILTER_FO), F32)], axis=0).astype(BF16)
    deltas = jnp.linspace(math.log(DECAY_TARGET) / DECAY_FAST, math.log(DECAY_TARGET) / DECAY_SLOW, D,
                          dtype=F32).reshape(1, D)
    return (winr, binr, wshr, bshr, fw1, f_b1.reshape(1, -1), f_w2.astype(BF16), f_b2.reshape(1, -1),
            f_w3.astype(BF16), sin_freq.reshape(1, -1), deltas, skip.reshape(1, D), w_out.astype(BF16),
            b_out.reshape(1, D))


def _hyena_filter(L, hw):
    (_, _, _, _, fw1, fb1, fw2, fb2, fw3, fr, deltas, _, _, _) = hw
    D = deltas.shape[1]
    tc = CH_TILE
    nc = D // tc
    t, z = _filter_features(L)
    cos_t, sin_t = _dft_tables(L)
    const = lambda c: (0, 0)
    kc, ks, kny = pl.pallas_call(
        functools.partial(_hy_filter_kernel, L=L),
        out_shape=(jax.ShapeDtypeStruct((L, D), F32), jax.ShapeDtypeStruct((L, D), F32),
                   jax.ShapeDtypeStruct((1, D), F32)),
        grid=(nc,),
        in_specs=[
            pl.BlockSpec(z.shape, const),
            pl.BlockSpec(t.shape, const),
            pl.BlockSpec((1, tc), lambda c: (0, c)),
            pl.BlockSpec(fw1.shape, const),
            pl.BlockSpec(fb1.shape, const),
            pl.BlockSpec(fw2.shape, const),
            pl.BlockSpec(fb2.shape, const),
            pl.BlockSpec((FILTER_FO, tc), lambda c: (0, c)),
            pl.BlockSpec((FILTER_FO, tc), lambda c: (0, nc + c)),
            pl.BlockSpec(fr.shape, const),
            pl.BlockSpec((L, L), const),
            pl.BlockSpec((L, L), const),
        ],
        out_specs=(pl.BlockSpec((L, tc), lambda c: (0, c)), pl.BlockSpec((L, tc), lambda c: (0, c)),
                   pl.BlockSpec((1, tc), lambda c: (0, c))),
        compiler_params=_cparams(("arbitrary",), VMEM_BIG),
        name="hyena_filter",
    )(z, t, deltas, fw1, fb1, fw2, fb2, fw3, fw3, fr, cos_t, sin_t)
    return cos_t, sin_t, kc, ks, kny


LCONV_TC = 512
LCONV_TF = 512


def _hyena(x, mod, gain, hw, filt):
    (winr, binr, wshr, bshr, _, _, _, _, _, _, _, skip, w_out, b_out) = hw
    cos_t, sin_t, kc, ks, kny = filt
    B, L, D = x.shape
    tc = CH_TILE
    nc = D // tc
    mod_b = mod.shape[0] > 1
    x0, vx = pl.pallas_call(
        functools.partial(_hy_in_kernel, L=L, tc=tc),
        out_shape=(jax.ShapeDtypeStruct((B, L, D), F32), jax.ShapeDtypeStruct((B, L, D), F32)),
        grid=(B, nc),
        in_specs=[
            pl.BlockSpec((1, L, D), lambda b, c: (b, 0, 0)),
            pl.BlockSpec((1, 6, D), (lambda b, c: (b, 0, 0)) if mod_b else (lambda b, c: (0, 0, 0))),
            pl.BlockSpec((1, D), lambda b, c: (0, 0)),
            pl.BlockSpec((1, D, 3 * tc), lambda b, c: (c, 0, 0)),
            pl.BlockSpec((1, 1, 3 * tc), lambda b, c: (c, 0, 0)),
            pl.BlockSpec((1, 3, 3 * tc), lambda b, c: (c, 0, 0)),
            pl.BlockSpec((1, 1, 3 * tc), lambda b, c: (c, 0, 0)),
        ],
        out_specs=(pl.BlockSpec((1, L, tc), lambda b, c: (b, 0, c)), pl.BlockSpec((1, L, tc), lambda b, c: (b, 0, c))),
        scratch_shapes=[pltpu.VMEM((L, D), BF16), pltpu.VMEM((L + 2 * HALO, 3 * tc), F32)],
        compiler_params=_cparams(("parallel", "arbitrary"), VMEM_BIG),
        name="hyena_in_shortconv",
    )(x, mod, gain, winr, binr, wshr, bshr)

    tcl = min(LCONV_TC, D)
    tfq = min(LCONV_TF, L)
    y = pl.pallas_call(
        functools.partial(_hy_conv_kernel, L=L),
        out_shape=jax.ShapeDtypeStruct((B, L, D), BF16),
        grid=(B, D // tcl, L // tfq),
        in_specs=[
            pl.BlockSpec((1, L, tcl), lambda b, c, f: (b, 0, c)),
            pl.BlockSpec((1, L, tcl), lambda b, c, f: (b, 0, c)),
            pl.BlockSpec((tfq, L), lambda b, c, f: (f, 0)),
            pl.BlockSpec((tfq, L), lambda b, c, f: (f, 0)),
            pl.BlockSpec((L, tfq), lambda b, c, f: (0, f)),
            pl.BlockSpec((L, tfq), lambda b, c, f: (0, f)),
            pl.BlockSpec((tfq, tcl), lambda b, c, f: (f, c)),
            pl.BlockSpec((tfq, tcl), lambda b, c, f: (f, c)),
            pl.BlockSpec((1, tcl), lambda b, c, f: (0, c)),
            pl.BlockSpec((1, tcl), lambda b, c, f: (0, c)),
        ],
        out_specs=pl.BlockSpec((1, L, tcl), lambda b, c, f: (b, 0, c)),
        scratch_shapes=[pltpu.VMEM((L, tcl), BF16), pltpu.VMEM((L, tcl), F32)],
        compiler_params=_cparams(("parallel", "parallel", "arbitrary"), VMEM_BIG),
        name="hyena_longconv",
    )(vx, x0, cos_t, sin_t, cos_t, sin_t, kc, ks, kny, skip)
    return _proj_resid(y, w_out, b_out, x, mod, 2, 512)


def kernel(x, c, ctx, c_ctx, ada_w, ada_b, norm_mix, norm_ffn, mla_w_dq, mla_q_norm, mla_w_uq, mla_w_dkv, mla_kv_norm, mla_w_ukv, mla_qk_gain, mla_w_o, cf_w_pw1, cf_b_pw1, cf_w_dw, cf_b_dw, cf_ln_g, cf_ln_b, cf_w_pw2, cf_b_pw2, hy_w_in, hy_b_in, hy_w_short, hy_b_short, hy_f_w1, hy_f_b1, hy_f_w2, hy_f_b2, hy_f_w3, hy_sin_freq, hy_skip, hy_w_out, hy_b_out, ffn_w_up, ffn_w_dw, ffn_b_dw, ffn_w_down):
    B, L, D = x.shape
    Lc = ctx.shape[1]
    depth = ada_w.shape[0]

    rows = ((B + 1 + 7) // 8) * 8
    cvec = jnp.concatenate([c, c_ctx[None, :], jnp.zeros((rows - B - 1, D), F32)], axis=0)
    ada = _ada_all(cvec, ada_w, ada_b)
    zero_bias = jnp.zeros((1, D), F32)
    rope = None

    for i in range(depth):
        kind = i % N_MIXERS
        j = i // N_MIXERS
        need_ctx_out = i < depth - 1
        modl = ada[i, :B].reshape(B, 6, D)
        modc = ada[i, B:B + 1].reshape(1, 6, D)
        gmix = norm_mix[i].reshape(1, D)
        gffn = norm_ffn[i].reshape(1, D)
        xc = None
        if kind == 0:
            mw = _prep_mla(mla_w_dq[j], mla_q_norm[j], mla_w_uq[j], mla_w_dkv[j], mla_kv_norm[j], mla_w_ukv[j],
                           mla_qk_gain[j], mla_w_o[j])
            if rope is None:
                rope = _rope_table(Lc, L)
            q, k, v = _mla_prep(ctx, x, modl, modc, gmix, mw, rope)
            o = _attention(q, k, v, 0, L, ATTN_TQ, 0, Lc + L)
            x_new = _proj_resid(o, mw[6], zero_bias, x, modl, 2, 512)
            if need_ctx_out:
                oc = _attention(q, k, v, L // Lc, Lc, Lc, L // Lc, Lc)
                xc = _proj_resid(oc, mw[6], zero_bias, ctx, modc, 2, 512)
            x = x_new
        elif kind == 1:
            cw = _prep_conf(cf_w_pw1[j], cf_b_pw1[j], cf_w_dw[j], cf_b_dw[j], cf_ln_g[j], cf_ln_b[j], cf_w_pw2[j],
                            cf_b_pw2[j])
            x = _conformer(x, modl, gmix, cw)
            if need_ctx_out:
                xc = _conformer(ctx, modc, gmix, cw)
        else:
            hw = _prep_hyena(hy_w_in[j], hy_b_in[j], hy_w_short[j], hy_b_short[j], hy_f_w1[j], hy_f_b1[j],
                             hy_f_w2[j], hy_f_b2[j], hy_f_w3[j], hy_sin_freq[j], hy_skip[j], hy_w_out[j],
                             hy_b_out[j])
            x = _hyena(x, modl, gmix, hw, _hyena_filter(L, hw))
            if need_ctx_out:
                xc = _hyena(ctx, modc, gmix, hw, _hyena_filter(Lc, hw))
        fw = _prep_ffn(ffn_w_up[i], ffn_w_dw[i], ffn_b_dw[i], ffn_w_down[i])
        x = _ffn(x, modl, gffn, fw)
        if need_ctx_out:
            ctx = _ffn(xc, modc, gffn, fw)
    return x
```

```python
import functools
import math

import jax
import jax.numpy as jnp
from jax import lax
from jax.experimental import pallas as pl
from jax.experimental.pallas import tpu as pltpu

F32 = jnp.float32
BF16 = jnp.bfloat16

D_MODEL = 1024
DEPTH = 4
GRID_W = 64
N_MIXERS = 3
EPS = 1e-6
N_HEADS = 16
QK_NOPE = 64
QK_ROPE = 32
V_DIM = 64
Q_LORA = 256
KV_LORA = 128
ROPE_THETA = 10000.0
CONV_W = 31
POS_EMB = 33
FILTER_FO = 64
DECAY_FAST = 0.3
DECAY_SLOW = 1.5
DECAY_TARGET = 1e-2
D_FF = 2816

LANES = 128
HEAD_SLOT = 128
MIB = 1024 * 1024
VMEM_BIG = 58 * MIB
VMEM_MID = 48 * MIB


def _cparams(sem, vmem=VMEM_MID):
    return pltpu.CompilerParams(dimension_semantics=sem, vmem_limit_bytes=vmem)


def _sigmoid(x):
    return 1.0 / (1.0 + jnp.exp(-x))


def _silu(x):
    return x * _sigmoid(x)


def _rms_scale(x, n):
    return lax.rsqrt(jnp.sum(x * x, axis=-1, keepdims=True) * (1.0 / n) + EPS)


def _modulate(x, gain, shift, scale):
    y = x * _rms_scale(x, x.shape[-1]) * gain
    return y * (1.0 + scale) + shift


def _dot(a, b):
    return jnp.dot(a, b, preferred_element_type=F32)


def _ada_kernel(c_ref, w_ref, b_ref, o_ref):
    s = _silu(c_ref[...]).astype(BF16)
    o_ref[0] = _dot(s, w_ref[0].astype(BF16)) + b_ref[0]


def _ada_all(cvec, ada_w, ada_b):
    rows = cvec.shape[0]
    depth, d, n = ada_w.shape
    tn = 1536
    return pl.pallas_call(
        _ada_kernel,
        out_shape=jax.ShapeDtypeStruct((depth, rows, n), F32),
        grid=(depth, n // tn),
        in_specs=[
            pl.BlockSpec((rows, d), lambda i, j: (0, 0)),
            pl.BlockSpec((1, d, tn), lambda i, j: (i, 0, j)),
            pl.BlockSpec((1, 1, tn), lambda i, j: (i, 0, j)),
        ],
        out_specs=pl.BlockSpec((1, rows, tn), lambda i, j: (i, 0, j)),
        compiler_params=_cparams(("parallel", "parallel")),
        name="ada_mod",
    )(cvec, ada_w, ada_b.reshape(depth, 1, n))


def _proj_resid_kernel(a_ref, w_ref, b_ref, x_ref, mod_ref, o_ref, *, gate_row):
    y = _dot(a_ref[0], w_ref[...]) + b_ref[...]
    o_ref[0] = x_ref[0] + mod_ref[0, gate_row:gate_row + 1, :] * y


def _proj_resid(a, w, bias, x, mod, gate_row, tl):
    B, L, K = a.shape
    D = w.shape[1]
    tl = min(tl, L)
    mod_b = mod.shape[0] > 1
    return pl.pallas_call(
        functools.partial(_proj_resid_kernel, gate_row=gate_row),
        out_shape=jax.ShapeDtypeStruct((B, L, D), F32),
        grid=(B, L // tl),
        in_specs=[
            pl.BlockSpec((1, tl, K), lambda b, t: (b, t, 0)),
            pl.BlockSpec((K, D), lambda b, t: (0, 0)),
            pl.BlockSpec((1, D), lambda b, t: (0, 0)),
            pl.BlockSpec((1, tl, D), lambda b, t: (b, t, 0)),
            pl.BlockSpec((1, 6, D), (lambda b, t: (b, 0, 0)) if mod_b else (lambda b, t: (0, 0, 0))),
        ],
        out_specs=pl.BlockSpec((1, tl, D), lambda b, t: (b, t, 0)),
        compiler_params=_cparams(("parallel", "parallel")),
        name="proj_resid",
    )(a, w, bias, x, mod)


FFN_TILE = 256
HALO = 8


FFN_HALO = 16
FFN_ROWS = 512
FFN_BLOCK_ROWS = 2048


def _ffn_kernel(x_ref, mod_ref, gain_ref, wup_ref, wdw_ref, bdw_ref, wdn_ref, o_ref, h_ref, pad_ref,
                *, ns, Ls, R, tf, mod_per_seq):
    j = pl.program_id(1)
    last = pl.num_programs(1) - 1
    H = FFN_HALO
    stride = Ls + H
    D = x_ref.shape[-1]

    @pl.when(j == 0)
    def _():
        for s in range(ns):
            m = mod_ref[s if mod_per_seq else 0]
            h = _modulate(x_ref[s], gain_ref[...], m[3:4], m[4:5])
            h_ref[s * stride:s * stride + H, :] = jnp.zeros((H, D), BF16)
            h_ref[s * stride + H:(s + 1) * stride, :] = h.astype(BF16)
            o_ref[s] = jnp.zeros((Ls, D), F32)
        h_ref[ns * stride:ns * stride + H, :] = jnp.zeros((H, D), BF16)

    w = wdw_ref[0]
    w_gate = wup_ref[0, :, :tf]
    w_lin = wup_ref[0, :, tf:]
    chunks = [(s, c) for s in range(ns) for c in range(Ls // R)]
    n = len(chunks)

    def up(i):
        s, c = chunks[i]
        r0 = s * stride + c * R
        pad_ref[i] = _dot(h_ref[r0:r0 + R + 2 * H, :], w_gate)
        return _dot(h_ref[r0 + H:r0 + H + R, :], w_lin)

    def gate(i, lin):
        g = (pad_ref[i, H - 1:H - 1 + R, :] * w[0:1] + pad_ref[i, H:H + R, :] * w[1:2]
             + pad_ref[i, H + 1:H + 1 + R, :] * w[2:3] + bdw_ref[0])
        return (_silu(g) * lin).astype(BF16)

    def down(i, act):
        s, c = chunks[i]
        o_ref[s, c * R:(c + 1) * R, :] += _dot(act, wdn_ref[0])

    lin = {0: up(0)}
    if n > 1:
        lin[1] = up(1)
    act = {0: gate(0, lin.pop(0))}
    for i in range(n):
        if i + 2 < n:
            lin[i + 2] = up(i + 2)
        down(i, act.pop(i))
        if i + 1 < n:
            act[i + 1] = gate(i + 1, lin.pop(i + 1))

    @pl.when(j == last)
    def _():
        for s in range(ns):
            m = mod_ref[s if mod_per_seq else 0]
            o_ref[s] = x_ref[s] + m[5:6] * o_ref[s]


def _prep_ffn(w_up, w_dw, b_dw, w_down):
    tf = FFN_TILE
    D = w_up.shape[0]
    nj = D_FF // tf
    wup = w_up.reshape(D, 2, nj, tf).transpose(2, 0, 1, 3).reshape(nj, D, 2 * tf).astype(BF16)
    wdw = w_dw.reshape(3, nj, tf).transpose(1, 0, 2)
    bdw = b_dw.reshape(nj, 1, tf)
    wdn = w_down.reshape(nj, tf, D).astype(BF16)
    return wup, wdw, bdw, wdn


def _ffn(x, mod, gain, ffn_w):
    wup, wdw, bdw, wdn = ffn_w
    B, L, D = x.shape
    tf = FFN_TILE
    nj = D_FF // tf
    mod_b = mod.shape[0] > 1
    ns = math.gcd(B, max(1, FFN_BLOCK_ROWS // L))
    R = min(FFN_ROWS, L)
    H = FFN_HALO
    return pl.pallas_call(
        functools.partial(_ffn_kernel, ns=ns, Ls=L, R=R, tf=tf, mod_per_seq=mod_b),
        out_shape=jax.ShapeDtypeStruct((B, L, D), F32),
        grid=(B // ns, nj),
        in_specs=[
            pl.BlockSpec((ns, L, D), lambda b, j: (b, 0, 0)),
            pl.BlockSpec((ns, 6, D), lambda b, j: (b, 0, 0)) if mod_b else pl.BlockSpec((1, 6, D), lambda b, j: (0, 0, 0)),
            pl.BlockSpec((1, D), lambda b, j: (0, 0)),
            pl.BlockSpec((1, D, 2 * tf), lambda b, j: (j, 0, 0)),
            pl.BlockSpec((1, 3, tf), lambda b, j: (j, 0, 0)),
            pl.BlockSpec((1, 1, tf), lambda b, j: (j, 0, 0)),
            pl.BlockSpec((1, tf, D), lambda b, j: (j, 0, 0)),
        ],
        out_specs=pl.BlockSpec((ns, L, D), lambda b, j: (b, 0, 0)),
        scratch_shapes=[pltpu.VMEM((ns * (L + H) + H, D), BF16),
                        pltpu.VMEM((ns * (L // R), R + 2 * H, tf), F32)],
        compiler_params=_cparams(("parallel", "arbitrary"), VMEM_BIG),
        name="conv_ffn",
    )(x, mod, gain, wup, wdw, bdw, wdn)


MLA_TL = 256


def _rope_partner(j):
    return j + 8 if (j % 16) < 8 else j - 8


def _rope_table(n_ctx, L):
    rows = L // GRID_W
    row = jnp.repeat(jnp.arange(rows), GRID_W).astype(F32)
    col = jnp.tile(jnp.arange(GRID_W), rows).astype(F32)
    half = QK_ROPE // 2
    inv = ROPE_THETA ** (-(jnp.arange(0, half, 2, dtype=F32) / half))
    ang = jnp.concatenate([row[:, None] * inv, col[:, None] * inv], axis=-1)
    ang = jnp.concatenate([ang, jnp.zeros((n_ctx, QK_ROPE // 2), F32)], axis=0)
    idx = jnp.array([(j // 16) * 8 + (j % 8) for j in range(QK_ROPE)])
    sgn = jnp.array([-1.0 if (j % 16) < 8 else 1.0 for j in range(QK_ROPE)], F32)
    a = ang[:, idx]
    ones = jnp.ones((ang.shape[0], QK_NOPE), F32)
    return jnp.concatenate([ones, jnp.cos(a), jnp.sin(a) * sgn], axis=-1)


def _prep_mla(w_dq, q_norm, w_uq, w_dkv, kv_norm, w_ukv, qk_gain, w_o):
    H = N_HEADS
    perm = jnp.array([_rope_partner(j) for j in range(QK_ROPE)])
    D = w_dq.shape[0]
    w_kpe = w_dkv[:, KV_LORA:]
    wd = jnp.concatenate([w_dq, w_dkv[:, :KV_LORA], jnp.zeros((D, QK_NOPE), F32), w_kpe, w_kpe[:, perm]],
                         axis=1).astype(BF16)
    wq = w_uq.reshape(Q_LORA, H, QK_NOPE + QK_ROPE)
    wq_pe = wq[:, :, QK_NOPE:]
    wuq = jnp.concatenate([wq, wq_pe[:, :, perm]], axis=-1).reshape(Q_LORA, H * HEAD_SLOT).astype(BF16)
    wkv = w_ukv.reshape(KV_LORA, H, QK_NOPE + V_DIM)
    wk = jnp.concatenate([wkv[:, :, :QK_NOPE], jnp.zeros((KV_LORA, H, HEAD_SLOT - QK_NOPE), F32)], axis=-1)
    wuk = wk.reshape(KV_LORA, H * HEAD_SLOT).astype(BF16)
    wuv_t = wkv[:, :, QK_NOPE:].reshape(KV_LORA, H * V_DIM).T.astype(BF16)
    scale = (QK_NOPE + QK_ROPE) ** -0.5 * math.log2(math.e)
    gq = qk_gain[0]
    gq_slot = jnp.concatenate([gq, gq[QK_NOPE:][perm]]) * scale
    gk = qk_gain[1]
    gk_pe_slot = jnp.concatenate([jnp.zeros((QK_NOPE,), F32), gk[QK_NOPE:], gk[QK_NOPE:][perm]])
    gk_n_slot = jnp.concatenate([gk[:QK_NOPE], jnp.zeros((HEAD_SLOT - QK_NOPE,), F32)])
    gains = jnp.stack([gq_slot, gk_pe_slot, gk_n_slot], axis=0)
    return (wd, q_norm.reshape(1, Q_LORA), kv_norm.reshape(1, KV_LORA), wuq, wuk, wuv_t, gains, w_o.astype(BF16))


def _mla_prep_kernel(ctx_ref, x_ref, modl_ref, modc_ref, gain_ref, wd_ref, qn_ref, kvn_ref, wuq_ref, wuk_ref,
                     wuvt_ref, gains_ref, rope_ref, q_ref, k_ref, vt_ref):
    is_ctx = pl.program_id(1) == pl.num_programs(1) - 1
    xin = jnp.where(is_ctx, ctx_ref[0], x_ref[0])
    m = jnp.where(is_ctx, modc_ref[0], modl_ref[0])
    h = _modulate(xin, gain_ref[...], m[0:1], m[1:2]).astype(BF16)
    d = _dot(h, wd_ref[...])
    cq = d[:, :Q_LORA]
    cq = (cq * _rms_scale(cq, Q_LORA) * qn_ref[...]).astype(BF16)
    ckv = d[:, Q_LORA:Q_LORA + KV_LORA]
    ckv = (ckv * _rms_scale(ckv, KV_LORA) * kvn_ref[...]).astype(BF16)

    lane = lax.broadcasted_iota(jnp.int32, (1, HEAD_SLOT), 1)
    nope_mask = lane < QK_NOPE
    pe_mask = jnp.logical_and(lane >= QK_NOPE, lane < QK_NOPE + QK_ROPE)
    rope = rope_ref[...]
    g_q = gains_ref[0:1, :]
    g_kpe = gains_ref[1:2, :]
    g_kn = gains_ref[2:3, :]

    def rotate_slot(t):
        return jnp.where(pe_mask, t + pltpu.roll(t, HEAD_SLOT - QK_ROPE, 1), 0.0)

    kp = d[:, Q_LORA + KV_LORA:]
    r_kp = _rms_scale(jnp.where(pe_mask, kp, 0.0), QK_ROPE)
    kpe_slot = rotate_slot(kp * r_kp * g_kpe * rope)

    qraw = _dot(cq, wuq_ref[...])
    kvu = _dot(ckv, wuk_ref[...])
    vt_ref[0] = lax.dot_general(wuvt_ref[...], ckv, (((1,), (1,)), ((), ())),
                                preferred_element_type=F32).astype(BF16)
    for hd in range(N_HEADS):
        sl = slice(hd * HEAD_SLOT, (hd + 1) * HEAD_SLOT)
        r = qraw[:, sl]
        rr = jnp.where(nope_mask, _rms_scale(jnp.where(nope_mask, r, 0.0), QK_NOPE),
                       _rms_scale(jnp.where(pe_mask, r, 0.0), QK_ROPE))
        t = r * rr * g_q * rope
        qh = jnp.where(nope_mask, t, rotate_slot(t))
        q_ref[0, :, sl] = qh.astype(BF16)
        kn = kvu[:, sl]
        kh = jnp.where(nope_mask, kn * _rms_scale(kn, QK_NOPE) * g_kn, kpe_slot)
        k_ref[0, :, sl] = kh.astype(BF16)


def _mla_prep(ctx, x, modl, modc, gain, mw, rope):
    wd, qn, kvn, wuq, wuk, wuv_t, gains, _ = mw
    B, L, D = x.shape
    Lc = ctx.shape[1]
    tl = MLA_TL
    assert Lc == tl and L % tl == 0
    nt = (Lc + L) // tl
    Lt = Lc + L
    H = N_HEADS
    const = lambda b, t: (0, 0)
    return pl.pallas_call(
        _mla_prep_kernel,
        out_shape=(jax.ShapeDtypeStruct((B, Lt, H * HEAD_SLOT), BF16),
                   jax.ShapeDtypeStruct((B, Lt, H * HEAD_SLOT), BF16),
                   jax.ShapeDtypeStruct((B, H * V_DIM, Lt), BF16)),
        grid=(B, nt),
        in_specs=[
            pl.BlockSpec((1, tl, D), lambda b, t: (b, 0, 0)),
            pl.BlockSpec((1, tl, D), lambda b, t: (b, jnp.minimum(t, nt - 2), 0)),
            pl.BlockSpec((1, 6, D), lambda b, t: (b, 0, 0)),
            pl.BlockSpec((1, 6, D), lambda b, t: (0, 0, 0)),
            pl.BlockSpec((1, D), const),
            pl.BlockSpec(wd.shape, const),
            pl.BlockSpec(qn.shape, const),
            pl.BlockSpec(kvn.shape, const),
            pl.BlockSpec(wuq.shape, const),
            pl.BlockSpec(wuk.shape, const),
            pl.BlockSpec(wuv_t.shape, const),
            pl.BlockSpec(gains.shape, const),
            pl.BlockSpec((tl, HEAD_SLOT), lambda b, t: (t, 0)),
        ],
        out_specs=(pl.BlockSpec((1, tl, H * HEAD_SLOT), lambda b, t: (b, t, 0)),
                   pl.BlockSpec((1, tl, H * HEAD_SLOT), lambda b, t: (b, t, 0)),
                   pl.BlockSpec((1, H * V_DIM, tl), lambda b, t: (b, 0, t))),
        compiler_params=_cparams(("parallel", "parallel")),
        name="mla_prep",
    )(ctx, x, modl, modc, gain, wd, qn, kvn, wuq, wuk, wuv_t, gains, rope)


HEADS_PER_STEP = 2
ATTN_TQ = 1024


ATTN_QCOLS = 256


def _attn_kernel(q_ref, k_ref, vt_ref, o_ref):
    tq = q_ref.shape[1]
    chains = [(qc, hh) for qc in range(tq // ATTN_QCOLS) for hh in range(HEADS_PER_STEP)]
    n = len(chains)

    def scores(qc, hh):
        sl = slice(hh * HEAD_SLOT, (hh + 1) * HEAD_SLOT)
        return lax.dot_general(k_ref[0, :, sl], q_ref[0, qc * ATTN_QCOLS:(qc + 1) * ATTN_QCOLS, sl],
                               (((1,), (1,)), ((), ())), preferred_element_type=F32)

    def softmax(st):
        p = jnp.exp2(st - jnp.max(st, axis=0, keepdims=True))
        return p.astype(BF16), jnp.sum(p, axis=0, keepdims=True)

    def weighted_values(hh, p, l):
        return _dot(vt_ref[0, hh * V_DIM:(hh + 1) * V_DIM, :], p) / l

    st = {0: scores(*chains[0])}
    if n > 1:
        st[1] = scores(*chains[1])
    sm = {0: softmax(st.pop(0))}
    done = {}
    for i in range(n):
        if i + 2 < n:
            st[i + 2] = scores(*chains[i + 2])
        done[chains[i]] = weighted_values(chains[i][1], *sm.pop(i))
        if i + 1 < n:
            sm[i + 1] = softmax(st.pop(i + 1))
        qc, hh = chains[i]
        if hh == HEADS_PER_STEP - 1:
            ot = jnp.concatenate([done.pop((qc, h2)) for h2 in range(HEADS_PER_STEP)], axis=0)
            o_ref[0, qc * ATTN_QCOLS:(qc + 1) * ATTN_QCOLS, :] = ot.T.astype(o_ref.dtype)


def _attention(q, k, vt, q_tile0, n_q, tq, k_tile0, n_k):
    B = q.shape[0]
    H = N_HEADS
    nhp = H // HEADS_PER_STEP
    wq = HEADS_PER_STEP * HEAD_SLOT
    wv = HEADS_PER_STEP * V_DIM
    return pl.pallas_call(
        _attn_kernel,
        out_shape=jax.ShapeDtypeStruct((B, n_q, H * V_DIM), BF16),
        grid=(B, nhp, n_q // tq),
        in_specs=[
            pl.BlockSpec((1, tq, wq), lambda b, h, t: (b, q_tile0 + t, h)),
            pl.BlockSpec((1, n_k, wq), lambda b, h, t: (b, k_tile0, h)),
            pl.BlockSpec((1, wv, n_k), lambda b, h, t: (b, h, k_tile0)),
        ],
        out_specs=pl.BlockSpec((1, tq, wv), lambda b, h, t: (b, t, h)),
        compiler_params=_cparams(("parallel", "parallel", "arbitrary")),
        name="mla_attn",
    )(q, k, vt)


CH_TILE = 256
CONV_PAD = 16
CONV_ROWS = 128


def _conf_a_kernel(x_ref, mod_ref, gain_ref, w1_ref, b1_ref, wdw_ref, bdw_ref, o_ref, h_ref, pad_ref, *, L, tc):
    c = pl.program_id(1)

    @pl.when(c == 0)
    def _():
        m = mod_ref[0]
        h_ref[...] = _modulate(x_ref[0], gain_ref[...], m[0:1], m[1:2]).astype(BF16)
        pad_ref[0:CONV_PAD, :] = jnp.zeros((CONV_PAD, tc), F32)
        pad_ref[CONV_PAD + L:2 * CONV_PAD + L, :] = jnp.zeros((CONV_PAD, tc), F32)

    a = _dot(h_ref[...], w1_ref[0]) + b1_ref[0]
    pad_ref[CONV_PAD:CONV_PAD + L, :] = a[:, :tc] * _sigmoid(a[:, tc:])
    w = wdw_ref[...]
    bias = bdw_ref[...]
    half = (CONV_W - 1) // 2
    rows = min(CONV_ROWS, L)
    for r0 in range(0, L, rows):
        acc = jnp.broadcast_to(bias, (rows, tc))
        for k in range(CONV_W):
            start = CONV_PAD - half + k + r0
            acc = acc + pad_ref[start:start + rows, :] * w[k:k + 1]
        o_ref[0, r0:r0 + rows, :] = acc


def _conf_b_kernel(u_ref, lng_ref, lnb_ref, w2_ref, b2_ref, x_ref, mod_ref, o_ref):
    u = u_ref[0]
    mu = jnp.mean(u, axis=-1, keepdims=True)
    uc = u - mu
    var = jnp.mean(uc * uc, axis=-1, keepdims=True)
    y = uc * lax.rsqrt(var + EPS) * lng_ref[...] + lnb_ref[...]
    y = _dot(_silu(y).astype(BF16), w2_ref[...]) + b2_ref[...]
    o_ref[0] = x_ref[0] + mod_ref[0, 2:3, :] * y


def _prep_conf(w1, b1, wdw, bdw, lng, lnb, w2, b2):
    D = w1.shape[0]
    tc = CH_TILE
    nc = D // tc
    w1r = w1.reshape(D, 2, nc, tc).transpose(2, 0, 1, 3).reshape(nc, D, 2 * tc).astype(BF16)
    b1r = b1.reshape(2, nc, tc).transpose(1, 0, 2).reshape(nc, 1, 2 * tc)
    return (w1r, b1r, wdw, bdw.reshape(1, D), lng.reshape(1, D), lnb.reshape(1, D), w2.astype(BF16),
            b2.reshape(1, D))


def _conformer(x, mod, gain, cw):
    w1r, b1r, wdw, bdw, lng, lnb, w2, b2 = cw
    B, L, D = x.shape
    tc = CH_TILE
    nc = D // tc
    mod_b = mod.shape[0] > 1
    mod_map2 = (lambda b, c: (b, 0, 0)) if mod_b else (lambda b, c: (0, 0, 0))
    u = pl.pallas_call(
        functools.partial(_conf_a_kernel, L=L, tc=tc),
        out_shape=jax.ShapeDtypeStruct((B, L, D), F32),
        grid=(B, nc),
        in_specs=[
            pl.BlockSpec((1, L, D), lambda b, c: (b, 0, 0)),
            pl.BlockSpec((1, 6, D), mod_map2),
            pl.BlockSpec((1, D), lambda b, c: (0, 0)),
            pl.BlockSpec((1, D, 2 * tc), lambda b, c: (c, 0, 0)),
            pl.BlockSpec((1, 1, 2 * tc), lambda b, c: (c, 0, 0)),
            pl.BlockSpec((CONV_W, tc), lambda b, c: (0, c)),
            pl.BlockSpec((1, tc), lambda b, c: (0, c)),
        ],
        out_specs=pl.BlockSpec((1, L, tc), lambda b, c: (b, 0, c)),
        scratch_shapes=[pltpu.VMEM((L, D), BF16), pltpu.VMEM((L + 2 * CONV_PAD, tc), F32)],
        compiler_params=_cparams(("parallel", "arbitrary")),
        name="conformer_glu_dwconv",
    )(x, mod, gain, w1r, b1r, wdw, bdw)
    tl = min(512, L)
    const = lambda b, t: (0, 0)
    return pl.pallas_call(
        _conf_b_kernel,
        out_shape=jax.ShapeDtypeStruct((B, L, D), F32),
        grid=(B, L // tl),
        in_specs=[
            pl.BlockSpec((1, tl, D), lambda b, t: (b, t, 0)),
            pl.BlockSpec((1, D), const),
            pl.BlockSpec((1, D), const),
            pl.BlockSpec((D, D), const),
            pl.BlockSpec((1, D), const),
            pl.BlockSpec((1, tl, D), lambda b, t: (b, t, 0)),
            pl.BlockSpec((1, 6, D), (lambda b, t: (b, 0, 0)) if mod_b else (lambda b, t: (0, 0, 0))),
        ],
        out_specs=pl.BlockSpec((1, tl, D), lambda b, t: (b, t, 0)),
        compiler_params=_cparams(("parallel", "parallel")),
        name="conformer_ln_pw2",
    )(u, lng, lnb, w2, b2, x, mod)


def _dft_tables(L):
    idx = jnp.arange(L, dtype=jnp.int32)
    prod = (idx[:, None] * idx[None, :]) % (2 * L)
    ang = prod.astype(F32) * (math.pi / L)
    return jnp.cos(ang).astype(BF16), jnp.sin(ang).astype(BF16)


def _filter_features(L):
    t = jnp.linspace(0.0, 1.0, L, dtype=F32)[:, None]
    bands = (POS_EMB - 1) // 2
    w = 2.0 * math.pi * jnp.arange(L, dtype=F32) / L
    f = jnp.linspace(1e-4, bands - 1, bands, dtype=F32)
    fw = w[:, None] * f[None, :]
    z = jnp.concatenate([t, jnp.cos(fw), -jnp.sin(fw)], axis=-1)
    z = jnp.concatenate([z, jnp.zeros((L, LANES - POS_EMB), F32)], axis=-1)
    return t, z.astype(BF16)


def _hy_filter_kernel(z_ref, t_ref, dl_ref, w1_ref, b1_ref, w2_ref, b2_ref, w3f_ref, w3b_ref, fr_ref,
                      cos_ref, sin_ref, kc_ref, ks_ref, kny_ref, *, L):
    fr = fr_ref[...]
    hdn = jnp.sin(fr * (_dot(z_ref[...], w1_ref[...]) + b1_ref[...]))
    hdn = jnp.sin(fr * (_dot(hdn.astype(BF16), w2_ref[...]) + b2_ref[...])).astype(BF16)
    decay = jnp.exp(-t_ref[...] * jnp.abs(dl_ref[...]))
    row = lax.broadcasted_iota(jnp.int32, (L, 1), 0)
    h_fwd = _dot(hdn, w3f_ref[...]) * decay
    h_bwd = jnp.where(row > 0, _dot(hdn, w3b_ref[...]) * decay, 0.0)
    nrm = (jnp.sum(jnp.abs(h_fwd), axis=0, keepdims=True) + jnp.sum(jnp.abs(h_bwd), axis=0, keepdims=True) + EPS)
    inv = 1.0 / nrm
    ksum = (h_fwd + h_bwd) * inv
    kdif = (h_fwd - h_bwd) * inv
    n = 2 * L
    wcol = jnp.where(row == 0, 1.0 / n, 2.0 / n)
    sgn = jnp.where(jnp.bitwise_and(row, 1) == 0, 1.0, -1.0)

    def split_dot(tab, kk):
        hi = kk.astype(BF16)
        lo = (kk - hi.astype(F32)).astype(BF16)
        return _dot(tab, hi) + _dot(tab, lo)

    kc_ref[...] = split_dot(cos_ref[...], ksum) * wcol
    ks_ref[...] = -split_dot(sin_ref[...], kdif) * wcol
    kny_ref[...] = jnp.sum(sgn * ksum, axis=0, keepdims=True) * (1.0 / n)


def _hy_in_kernel(x_ref, mod_ref, gain_ref, win_ref, bin_ref, wsh_ref, bsh_ref, x0_ref, vx_ref, h_ref, pad_ref,
                  *, L, tc):
    c = pl.program_id(1)

    @pl.when(c == 0)
    def _():
        m = mod_ref[0]
        h_ref[...] = _modulate(x_ref[0], gain_ref[...], m[0:1], m[1:2]).astype(BF16)
        pad_ref[0:HALO, :] = jnp.zeros((HALO, 3 * tc), F32)
        pad_ref[HALO + L:2 * HALO + L, :] = jnp.zeros((HALO, 3 * tc), F32)

    pad_ref[HALO:HALO + L, :] = _dot(h_ref[...], win_ref[0]) + bin_ref[0]
    w = wsh_ref[0]
    u = (pad_ref[HALO - 1:HALO - 1 + L, :] * w[0:1] + pad_ref[HALO:HALO + L, :] * w[1:2]
         + pad_ref[HALO + 1:HALO + 1 + L, :] * w[2:3] + bsh_ref[0])
    x0_ref[0] = u[:, :tc]
    vx_ref[0] = u[:, 2 * tc:] * u[:, tc:2 * tc]


def _hy_conv_kernel(vx_ref, x0_ref, cr_ref, sr_ref, cc_ref, sc_ref, kc_ref, ks_ref, kny_ref, skip_ref, o_ref,
                    u_ref, acc_ref, *, L):
    f = pl.program_id(2)
    last = pl.num_programs(2) - 1

    @pl.when(f == 0)
    def _():
        u_ref[...] = vx_ref[0].astype(BF16)

    u = u_ref[...]
    pc = _dot(cr_ref[...], u)
    ps = _dot(sr_ref[...], u)
    kc = kc_ref[...]
    ks = ks_ref[...]
    zc = (pc * kc + ps * ks).astype(BF16)
    zs = (ps * kc - pc * ks).astype(BF16)
    y = _dot(cc_ref[...], zc) + _dot(sc_ref[...], zs)

    @pl.when(f == 0)
    def _():
        acc_ref[...] = y

    @pl.when(f > 0)
    def _():
        acc_ref[...] += y

    @pl.when(f == last)
    def _():
        vx = vx_ref[0]
        row = lax.broadcasted_iota(jnp.int32, (L, 1), 0)
        sgn = jnp.where(jnp.bitwise_and(row, 1) == 0, 1.0, -1.0)
        u_ny = jnp.sum(sgn * vx, axis=0, keepdims=True)
        yy = acc_ref[...] + sgn * (u_ny * kny_ref[...])
        o_ref[0] = ((yy + skip_ref[...] * vx) * x0_ref[0]).astype(o_ref.dtype)


def _prep_hyena(w_in, b_in, w_short, b_short, f_w1, f_b1, f_w2, f_b2, f_w3, sin_freq, skip, w_out, b_out):
    D = w_in.shape[0]
    tc = CH_TILE
    nc = D // tc
    winr = w_in.reshape(D, 3, nc, tc).transpose(2, 0, 1, 3).reshape(nc, D, 3 * tc).astype(BF16)
    binr = b_in.reshape(3, nc, tc).transpose(1, 0, 2).reshape(nc, 1, 3 * tc)
    wshr = w_short.reshape(3, 3, nc, tc).transpose(2, 0, 1, 3).reshape(nc, 3, 3 * tc)
    bshr = b_short.reshape(3, nc, tc).transpose(1, 0, 2).reshape(nc, 1, 3 * tc)
    fw1 = jnp.concatenate([f_w1, jnp.zeros((LANES - POS_EMB, FILTER_FO), F32)], axis=0).astype(BF16)
    deltas = jnp.linspace(math.log(DECAY_TARGET) / DECAY_FAST, math.log(DECAY_TARGET) / DECAY_SLOW, D,
                          dtype=F32).reshape(1, D)
    return (winr, binr, wshr, bshr, fw1, f_b1.reshape(1, -1), f_w2.astype(BF16), f_b2.reshape(1, -1),
            f_w3.astype(BF16), sin_freq.reshape(1, -1), deltas, skip.reshape(1, D), w_out.astype(BF16),
            b_out.reshape(1, D))


def _hyena_filter(L, hw):
    (_, _, _, _, fw1, fb1, fw2, fb2, fw3, fr, deltas, _, _, _) = hw
    D = deltas.shape[1]
    tc = CH_TILE
    nc = D // tc
    t, z = _filter_features(L)
    cos_t, sin_t = _dft_tables(L)
    const = lambda c: (0, 0)
    kc, ks, kny = pl.pallas_call(
        functools.partial(_hy_filter_kernel, L=L),
        out_shape=(jax.ShapeDtypeStruct((L, D), F32), jax.ShapeDtypeStruct((L, D), F32),
                   jax.ShapeDtypeStruct((1, D), F32)),
        grid=(nc,),
        in_specs=[
            pl.BlockSpec(z.shape, const),
            pl.BlockSpec(t.shape, const),
            pl.BlockSpec((1, tc), lambda c: (0, c)),
            pl.BlockSpec(fw1.shape, const),
            pl.BlockSpec(fb1.shape, const),
            pl.BlockSpec(fw2.shape, const),
            pl.BlockSpec(fb2.shape, const),
            pl.BlockSpec((FILTER_FO, tc), lambda c: (0, c)),
            pl.BlockSpec((FILTER_FO, tc), lambda c: (0, nc + c)),
            pl.BlockSpec(fr.shape, const),
            pl.BlockSpec((L, L), const),
            pl.BlockSpec((L, L), const),
        ],
        out_specs=(pl.BlockSpec((L, tc), lambda c: (0, c)), pl.BlockSpec((L, tc), lambda c: (0, c)),
                   pl.BlockSpec((1, tc), lambda c: (0, c))),
        compiler_params=_cparams(("arbitrary",), VMEM_BIG),
        name="hyena_filter",
    )(z, t, deltas, fw1, fb1, fw2, fb2, fw3, fw3, fr, cos_t, sin_t)
    return cos_t, sin_t, kc, ks, kny


LCONV_TC = 512
LCONV_TF = 512


def _hyena(x, mod, gain, hw, filt):
    (winr, binr, wshr, bshr, _, _, _, _, _, _, _, skip, w_out, b_out) = hw
    cos_t, sin_t, kc, ks, kny = filt
    B, L, D = x.shape
    tc = CH_TILE
    nc = D // tc
    mod_b = mod.shape[0] > 1
    x0, vx = pl.pallas_call(
        functools.partial(_hy_in_kernel, L=L, tc=tc),
        out_shape=(jax.ShapeDtypeStruct((B, L, D), F32), jax.ShapeDtypeStruct((B, L, D), F32)),
        grid=(B, nc),
        in_specs=[
            pl.BlockSpec((1, L, D), lambda b, c: (b, 0, 0)),
            pl.BlockSpec((1, 6, D), (lambda b, c: (b, 0, 0)) if mod_b else (lambda b, c: (0, 0, 0))),
            pl.BlockSpec((1, D), lambda b, c: (0, 0)),
            pl.BlockSpec((1, D, 3 * tc), lambda b, c: (c, 0, 0)),
            pl.BlockSpec((1, 1, 3 * tc), lambda b, c: (c, 0, 0)),
            pl.BlockSpec((1, 3, 3 * tc), lambda b, c: (c, 0, 0)),
            pl.BlockSpec((1, 1, 3 * tc), lambda b, c: (c, 0, 0)),
        ],
        out_specs=(pl.BlockSpec((1, L, tc), lambda b, c: (b, 0, c)), pl.BlockSpec((1, L, tc), lambda b, c: (b, 0, c))),
        scratch_shapes=[pltpu.VMEM((L, D), BF16), pltpu.VMEM((L + 2 * HALO, 3 * tc), F32)],
        compiler_params=_cparams(("parallel", "arbitrary"), VMEM_BIG),
        name="hyena_in_shortconv",
    )(x, mod, gain, winr, binr, wshr, bshr)

    tcl = min(LCONV_TC, D)
    tfq = min(LCONV_TF, L)
    y = pl.pallas_call(
        functools.partial(_hy_conv_kernel, L=L),
        out_shape=jax.ShapeDtypeStruct((B, L, D), BF16),
        grid=(B, D // tcl, L // tfq),
        in_specs=[
            pl.BlockSpec((1, L, tcl), lambda b, c, f: (b, 0, c)),
            pl.BlockSpec((1, L, tcl), lambda b, c, f: (b, 0, c)),
            pl.BlockSpec((tfq, L), lambda b, c, f: (f, 0)),
            pl.BlockSpec((tfq, L), lambda b, c, f: (f, 0)),
            pl.BlockSpec((L, tfq), lambda b, c, f: (0, f)),
            pl.BlockSpec((L, tfq), lambda b, c, f: (0, f)),
            pl.BlockSpec((tfq, tcl), lambda b, c, f: (f, c)),
            pl.BlockSpec((tfq, tcl), lambda b, c, f: (f, c)),
            pl.BlockSpec((1, tcl), lambda b, c, f: (0, c)),
            pl.BlockSpec((1, tcl), lambda b, c, f: (0, c)),
        ],
        out_specs=pl.BlockSpec((1, L, tcl), lambda b, c, f: (b, 0, c)),
        scratch_shapes=[pltpu.VMEM((L, tcl), BF16), pltpu.VMEM((L, tcl), F32)],
        compiler_params=_cparams(("parallel", "parallel", "arbitrary"), VMEM_BIG),
        name="hyena_longconv",
    )(vx, x0, cos_t, sin_t, cos_t, sin_t, kc, ks, kny, skip)
    return _proj_resid(y, w_out, b_out, x, mod, 2, 512)


def kernel(x, c, ctx, c_ctx, ada_w, ada_b, norm_mix, norm_ffn, mla_w_dq, mla_q_norm, mla_w_uq, mla_w_dkv, mla_kv_norm, mla_w_ukv, mla_qk_gain, mla_w_o, cf_w_pw1, cf_b_pw1, cf_w_dw, cf_b_dw, cf_ln_g, cf_ln_b, cf_w_pw2, cf_b_pw2, hy_w_in, hy_b_in, hy_w_short, hy_b_short, hy_f_w1, hy_f_b1, hy_f_w2, hy_f_b2, hy_f_w3, hy_sin_freq, hy_skip, hy_w_out, hy_b_out, ffn_w_up, ffn_w_dw, ffn_b_dw, ffn_w_down):
    B, L, D = x.shape
    Lc = ctx.shape[1]
    depth = ada_w.shape[0]

    rows = ((B + 1 + 7) // 8) * 8
    cvec = jnp.concatenate([c, c_ctx[None, :], jnp.zeros((rows - B - 1, D), F32)], axis=0)
    ada = _ada_all(cvec, ada_w, ada_b)
    zero_bias = jnp.zeros((1, D), F32)
    rope = None

    for i in range(depth):
        kind = i % N_MIXERS
        j = i // N_MIXERS
        need_ctx_out = i < depth - 1
        modl = ada[i, :B].reshape(B, 6, D)
        modc = ada[i, B:B + 1].reshape(1, 6, D)
        gmix = norm_mix[i].reshape(1, D)
        gffn = norm_ffn[i].reshape(1, D)
        xc = None
        if kind == 0:
            mw = _prep_mla(mla_w_dq[j], mla_q_norm[j], mla_w_uq[j], mla_w_dkv[j], mla_kv_norm[j], mla_w_ukv[j],
                           mla_qk_gain[j], mla_w_o[j])
            if rope is None:
                rope = _rope_table(Lc, L)
            q, k, vt = _mla_prep(ctx, x, modl, modc, gmix, mw, rope)
            o = _attention(q, k, vt, 0, L, ATTN_TQ, 0, Lc + L)
            x_new = _proj_resid(o, mw[-1], zero_bias, x, modl, 2, 512)
            if need_ctx_out:
                oc = _attention(q, k, vt, L // Lc, Lc, Lc, L // Lc, Lc)
                xc = _proj_resid(oc, mw[-1], zero_bias, ctx, modc, 2, 512)
            x = x_new
        elif kind == 1:
            cw = _prep_conf(cf_w_pw1[j], cf_b_pw1[j], cf_w_dw[j], cf_b_dw[j], cf_ln_g[j], cf_ln_b[j], cf_w_pw2[j],
                            cf_b_pw2[j])
            x = _conformer(x, modl, gmix, cw)
            if need_ctx_out:
                xc = _conformer(ctx, modc, gmix, cw)
        else:
            hw = _prep_hyena(hy_w_in[j], hy_b_in[j], hy_w_short[j], hy_b_short[j], hy_f_w1[j], hy_f_b1[j],
                             hy_f_w2[j], hy_f_b2[j], hy_f_w3[j], hy_sin_freq[j], hy_skip[j], hy_w_out[j],
                             hy_b_out[j])
            x = _hyena(x, modl, gmix, hw, _hyena_filter(L, hw))
            if need_ctx_out:
                xc = _hyena(ctx, modc, gmix, hw, _hyena_filter(Lc, hw))
        fw = _prep_ffn(ffn_w_up[i], ffn_w_dw[i], ffn_b_dw[i], ffn_w_down[i])
        x = _ffn(x, modl, gffn, fw)
        if need_ctx_out:
            ctx = _ffn(xc, modc, gffn, fw)
    return x
```

```python
import functools
import math

import numpy as np
import jax
import jax.numpy as jnp
from jax import lax
from jax.experimental import pallas as pl
from jax.experimental.pallas import tpu as pltpu

F32 = jnp.float32
BF16 = jnp.bfloat16

D_MODEL = 1024
DEPTH = 4
GRID_W = 64
N_MIXERS = 3
EPS = 1e-6
N_HEADS = 16
QK_NOPE = 64
QK_ROPE = 32
V_DIM = 64
Q_LORA = 256
KV_LORA = 128
ROPE_THETA = 10000.0
CONV_W = 31
POS_EMB = 33
FILTER_FO = 64
DECAY_FAST = 0.3
DECAY_SLOW = 1.5
DECAY_TARGET = 1e-2
D_FF = 2816

LANES = 128
HEAD_SLOT = 128
MIB = 1024 * 1024
VMEM_BIG = 58 * MIB
VMEM_MID = 48 * MIB


def _cparams(sem, vmem=VMEM_MID):
    return pltpu.CompilerParams(dimension_semantics=sem, vmem_limit_bytes=vmem)


def _sigmoid(x):
    return 1.0 / (1.0 + jnp.exp(-x))


def _silu(x):
    return x * _sigmoid(x)


def _rms_scale(x, n):
    return lax.rsqrt(jnp.sum(x * x, axis=-1, keepdims=True) * (1.0 / n) + EPS)


def _modulate(x, gain, shift, scale):
    y = x * _rms_scale(x, x.shape[-1]) * gain
    return y * (1.0 + scale) + shift


def _dot(a, b):
    return jnp.dot(a, b, preferred_element_type=F32)


def _ada_kernel(c_ref, w_ref, b_ref, o_ref):
    s = _silu(c_ref[...]).astype(BF16)
    o_ref[0] = _dot(s, w_ref[0].astype(BF16)) + b_ref[0]


def _ada_all(cvec, ada_w, ada_b):
    rows = cvec.shape[0]
    depth, d, n = ada_w.shape
    tn = 1536
    return pl.pallas_call(
        _ada_kernel,
        out_shape=jax.ShapeDtypeStruct((depth, rows, n), F32),
        grid=(depth, n // tn),
        in_specs=[
            pl.BlockSpec((rows, d), lambda i, j: (0, 0)),
            pl.BlockSpec((1, d, tn), lambda i, j: (i, 0, j)),
            pl.BlockSpec((1, 1, tn), lambda i, j: (i, 0, j)),
        ],
        out_specs=pl.BlockSpec((1, rows, tn), lambda i, j: (i, 0, j)),
        compiler_params=_cparams(("parallel", "parallel")),
        name="ada_mod",
    )(cvec, ada_w, ada_b.reshape(depth, 1, n))


def _proj_resid_kernel(a_ref, w_ref, b_ref, x_ref, mod_ref, o_ref, *, gate_row):
    y = _dot(a_ref[0], w_ref[...]) + b_ref[...]
    o_ref[0] = x_ref[0] + mod_ref[0, gate_row:gate_row + 1, :] * y


def _proj_resid(a, w, bias, x, mod, gate_row, tl):
    B, L, K = a.shape
    D = w.shape[1]
    tl = min(tl, L)
    mod_b = mod.shape[0] > 1
    return pl.pallas_call(
        functools.partial(_proj_resid_kernel, gate_row=gate_row),
        out_shape=jax.ShapeDtypeStruct((B, L, D), F32),
        grid=(B, L // tl),
        in_specs=[
            pl.BlockSpec((1, tl, K), lambda b, t: (b, t, 0)),
            pl.BlockSpec((K, D), lambda b, t: (0, 0)),
            pl.BlockSpec((1, D), lambda b, t: (0, 0)),
            pl.BlockSpec((1, tl, D), lambda b, t: (b, t, 0)),
            pl.BlockSpec((1, 6, D), (lambda b, t: (b, 0, 0)) if mod_b else (lambda b, t: (0, 0, 0))),
        ],
        out_specs=pl.BlockSpec((1, tl, D), lambda b, t: (b, t, 0)),
        compiler_params=_cparams(("parallel", "parallel")),
        name="proj_resid",
    )(a, w, bias, x, mod)


FFN_TILE = 256


FFN_HALO = 16
FFN_ROWS = 512
FFN_BLOCK_ROWS = 2048


def _ffn_kernel(x_ref, mod_ref, gain_ref, wg_ref, wl_ref, wdw_ref, bdw_ref, wdn_ref, o_ref, h_ref, pad_ref,
                *, ns, Ls, R, mod_per_seq):
    j = pl.program_id(1)
    last = pl.num_programs(1) - 1
    H = FFN_HALO
    stride = Ls + H
    D = x_ref.shape[-1]

    @pl.when(j == 0)
    def _():
        for s in range(ns):
            m = mod_ref[s if mod_per_seq else 0]
            h = _modulate(x_ref[s], gain_ref[...], m[3:4], m[4:5])
            h_ref[s * stride:s * stride + H, :] = jnp.zeros((H, D), BF16)
            h_ref[s * stride + H:(s + 1) * stride, :] = h.astype(BF16)
            o_ref[s] = jnp.zeros((Ls, D), F32)
        h_ref[ns * stride:ns * stride + H, :] = jnp.zeros((H, D), BF16)

    w = wdw_ref[0]
    w_gate = wg_ref[0]
    w_lin = wl_ref[0]
    chunks = [(s, c) for s in range(ns) for c in range(Ls // R)]
    n = len(chunks)

    def up(i):
        s, c = chunks[i]
        r0 = s * stride + c * R
        pad_ref[i] = _dot(h_ref[r0:r0 + R + 2 * H, :], w_gate)
        return _dot(h_ref[r0 + H:r0 + H + R, :], w_lin)

    def gate(i, lin):
        g = (pad_ref[i, H - 1:H - 1 + R, :] * w[0:1] + pad_ref[i, H:H + R, :] * w[1:2]
             + pad_ref[i, H + 1:H + 1 + R, :] * w[2:3] + bdw_ref[0])
        return (_silu(g) * lin).astype(BF16)

    def down(i, act):
        s, c = chunks[i]
        o_ref[s, c * R:(c + 1) * R, :] += _dot(act, wdn_ref[0])

    lin = {0: up(0)}
    if n > 1:
        lin[1] = up(1)
    act = {0: gate(0, lin.pop(0))}
    for i in range(n):
        if i + 2 < n:
            lin[i + 2] = up(i + 2)
        down(i, act.pop(i))
        if i + 1 < n:
            act[i + 1] = gate(i + 1, lin.pop(i + 1))

    @pl.when(j == last)
    def _():
        for s in range(ns):
            m = mod_ref[s if mod_per_seq else 0]
            o_ref[s] = x_ref[s] + m[5:6] * o_ref[s]


def _prep_ffn(w_up, w_dw, b_dw, w_down):
    depth = w_up.shape[0]
    return w_up.astype(BF16), w_dw, b_dw.reshape(depth, 1, D_FF), w_down.astype(BF16)


def _ffn(x, mod, gain, ffn_w, layer):
    wup, wdw, bdw, wdn = ffn_w
    B, L, D = x.shape
    tf = FFN_TILE
    nj = D_FF // tf
    mod_b = mod.shape[0] > 1
    ns = math.gcd(B, max(1, FFN_BLOCK_ROWS // L))
    R = min(FFN_ROWS, L)
    H = FFN_HALO
    return pl.pallas_call(
        functools.partial(_ffn_kernel, ns=ns, Ls=L, R=R, mod_per_seq=mod_b),
        out_shape=jax.ShapeDtypeStruct((B, L, D), F32),
        grid=(B // ns, nj),
        in_specs=[
            pl.BlockSpec((ns, L, D), lambda b, j: (b, 0, 0)),
            pl.BlockSpec((ns, 6, D), lambda b, j: (b, 0, 0)) if mod_b else pl.BlockSpec((1, 6, D), lambda b, j: (0, 0, 0)),
            pl.BlockSpec((1, D), lambda b, j: (0, 0)),
            pl.BlockSpec((1, D, tf), lambda b, j: (layer, 0, j)),
            pl.BlockSpec((1, D, tf), lambda b, j: (layer, 0, nj + j)),
            pl.BlockSpec((1, 3, tf), lambda b, j: (layer, 0, j)),
            pl.BlockSpec((1, 1, tf), lambda b, j: (layer, 0, j)),
            pl.BlockSpec((1, tf, D), lambda b, j: (layer, j, 0)),
        ],
        out_specs=pl.BlockSpec((ns, L, D), lambda b, j: (b, 0, 0)),
        scratch_shapes=[pltpu.VMEM((ns * (L + H) + H, D), BF16),
                        pltpu.VMEM((ns * (L // R), R + 2 * H, tf), F32)],
        compiler_params=_cparams(("parallel", "arbitrary"), VMEM_BIG),
        name="conv_ffn",
    )(x, mod, gain, wup, wup, wdw, bdw, wdn)


MLA_TL = 256


def _rope_partner(j):
    return j + 8 if (j % 16) < 8 else j - 8


def _rope_table(n_ctx, L):
    rows = L // GRID_W
    row = jnp.repeat(jnp.arange(rows), GRID_W).astype(F32)
    col = jnp.tile(jnp.arange(GRID_W), rows).astype(F32)
    half = QK_ROPE // 2
    inv = ROPE_THETA ** (-(jnp.arange(0, half, 2, dtype=F32) / half))
    ang = jnp.concatenate([row[:, None] * inv, col[:, None] * inv], axis=-1)
    ang = jnp.concatenate([ang, jnp.zeros((n_ctx, QK_ROPE // 2), F32)], axis=0)
    idx = jnp.array([(j // 16) * 8 + (j % 8) for j in range(QK_ROPE)])
    sgn = jnp.array([-1.0 if (j % 16) < 8 else 1.0 for j in range(QK_ROPE)], F32)
    a = ang[:, idx]
    ones = jnp.ones((ang.shape[0], QK_NOPE), F32)
    return jnp.concatenate([ones, jnp.cos(a), jnp.sin(a) * sgn], axis=-1)


def _prep_mla(w_dq, q_norm, w_uq, w_dkv, kv_norm, w_ukv, qk_gain, w_o):
    H = N_HEADS
    perm = jnp.array([_rope_partner(j) for j in range(QK_ROPE)])
    D = w_dq.shape[0]
    w_kpe = w_dkv[:, KV_LORA:]
    wd = jnp.concatenate([w_dq, w_dkv[:, :KV_LORA], jnp.zeros((D, QK_NOPE), F32), w_kpe, w_kpe[:, perm]],
                         axis=1).astype(BF16)
    wq = w_uq.reshape(Q_LORA, H, QK_NOPE + QK_ROPE)
    wq_pe = wq[:, :, QK_NOPE:]
    wuq = jnp.concatenate([wq, wq_pe[:, :, perm]], axis=-1).reshape(Q_LORA, H * HEAD_SLOT).astype(BF16)
    wkv = w_ukv.reshape(KV_LORA, H, QK_NOPE + V_DIM)
    wk = jnp.concatenate([wkv[:, :, :QK_NOPE], jnp.zeros((KV_LORA, H, HEAD_SLOT - QK_NOPE), F32)], axis=-1)
    wuk = wk.reshape(KV_LORA, H * HEAD_SLOT).astype(BF16)
    wuv_t = wkv[:, :, QK_NOPE:].reshape(KV_LORA, H * V_DIM).T.astype(BF16)
    scale = (QK_NOPE + QK_ROPE) ** -0.5 * math.log2(math.e)
    cn, cp = math.sqrt(QK_NOPE), math.sqrt(QK_ROPE)
    gq = qk_gain[0]
    gq_slot = jnp.concatenate([gq[:QK_NOPE] * cn, gq[QK_NOPE:] * cp, gq[QK_NOPE:][perm] * cp]) * scale
    gk = qk_gain[1]
    gk_pe_slot = jnp.concatenate([jnp.zeros((QK_NOPE,), F32), gk[QK_NOPE:], gk[QK_NOPE:][perm]]) * cp
    gk_n_slot = jnp.concatenate([gk[:QK_NOPE] * cn, jnp.zeros((HEAD_SLOT - QK_NOPE,), F32)])
    gains = jnp.stack([gq_slot, gk_pe_slot, gk_n_slot], axis=0)
    return (wd, q_norm.reshape(1, Q_LORA), kv_norm.reshape(1, KV_LORA), wuq, wuk, wuv_t, gains, w_o.astype(BF16))


def _mla_prep_kernel(ctx_ref, x_ref, modl_ref, modc_ref, gain_ref, wd_ref, qn_ref, kvn_ref, wuq_ref, wuk_ref,
                     wuvt_ref, gkn_ref, tab_ref, q_ref, k_ref, vt_ref):
    is_ctx = pl.program_id(1) == pl.num_programs(1) - 1
    xin = jnp.where(is_ctx, ctx_ref[0], x_ref[0])
    m = jnp.where(is_ctx, modc_ref[0], modl_ref[0])
    h = _modulate(xin, gain_ref[...], m[0:1], m[1:2]).astype(BF16)
    d = _dot(h, wd_ref[...])
    cq = d[:, :Q_LORA]
    cq = (cq * _rms_scale(cq, Q_LORA) * qn_ref[...]).astype(BF16)
    ckv = d[:, Q_LORA:Q_LORA + KV_LORA]
    ckv = (ckv * _rms_scale(ckv, KV_LORA) * kvn_ref[...]).astype(BF16)

    lane = lax.broadcasted_iota(jnp.int32, (1, HEAD_SLOT), 1)
    nope_mask = lane < QK_NOPE
    pe_mask = jnp.logical_and(lane >= QK_NOPE, lane < QK_NOPE + QK_ROPE)
    tq_a = tab_ref[:, 0 * HEAD_SLOT:1 * HEAD_SLOT]
    tq_b = tab_ref[:, 1 * HEAD_SLOT:2 * HEAD_SLOT]
    tk_a = tab_ref[:, 2 * HEAD_SLOT:3 * HEAD_SLOT]
    tk_b = tab_ref[:, 3 * HEAD_SLOT:4 * HEAD_SLOT]
    g_kn = gkn_ref[...]
    fold = HEAD_SLOT - QK_ROPE

    def sumsq(v):
        return jnp.sum(v * v, axis=-1, keepdims=True)

    kp = d[:, Q_LORA + KV_LORA:]
    xk = kp * lax.rsqrt(0.5 * sumsq(kp) + QK_ROPE * EPS)
    kpe_slot = xk * tk_a + pltpu.roll(xk * tk_b, fold, 1)

    qraw = _dot(cq, wuq_ref[...])
    kvu = _dot(ckv, wuk_ref[...])
    vt_ref[0] = lax.dot_general(wuvt_ref[...], ckv, (((1,), (1,)), ((), ())),
                                preferred_element_type=F32).astype(BF16)
    def norms(hd):
        sl = slice(hd * HEAD_SLOT, (hd + 1) * HEAD_SLOT)
        sq = qraw[:, sl] * qraw[:, sl]
        rn = lax.rsqrt(jnp.sum(jnp.where(nope_mask, sq, 0.0), axis=-1, keepdims=True) + QK_NOPE * EPS)
        rp = lax.rsqrt(jnp.sum(jnp.where(pe_mask, sq, 0.0), axis=-1, keepdims=True) + QK_ROPE * EPS)
        rk = lax.rsqrt(sumsq(kvu[:, sl]) + QK_NOPE * EPS)
        return rn, rp, rk

    def emit(hd, rn, rp, rk):
        sl = slice(hd * HEAD_SLOT, (hd + 1) * HEAD_SLOT)
        xq = qraw[:, sl] * jnp.where(nope_mask, rn, rp)
        q_ref[0, :, sl] = (xq * tq_a + pltpu.roll(xq * tq_b, fold, 1)).astype(BF16)
        k_ref[0, :, sl] = (kvu[:, sl] * rk * g_kn + kpe_slot).astype(BF16)

    ahead = 2
    stats = {hd: norms(hd) for hd in range(min(ahead, N_HEADS))}
    for hd in range(N_HEADS):
        if hd + ahead < N_HEADS:
            stats[hd + ahead] = norms(hd + ahead)
        emit(hd, *stats.pop(hd))


def _mla_prep(ctx, x, modl, modc, gain, mw, rope):
    wd, qn, kvn, wuq, wuk, wuv_t, gains, _ = mw
    B, L, D = x.shape
    Lc = ctx.shape[1]
    tl = MLA_TL
    assert Lc == tl and L % tl == 0
    nt = (Lc + L) // tl
    Lt = Lc + L
    H = N_HEADS
    const = lambda b, t: (0, 0)
    partner_lanes = jnp.arange(HEAD_SLOT) >= QK_NOPE + QK_ROPE
    tq = rope * gains[0:1]
    tk = rope * gains[1:2]
    tabs = jnp.concatenate([jnp.where(partner_lanes, 0.0, tq), jnp.where(partner_lanes, tq, 0.0),
                            jnp.where(partner_lanes, 0.0, tk), jnp.where(partner_lanes, tk, 0.0)], axis=1)
    gkn = gains[2:3]
    return pl.pallas_call(
        _mla_prep_kernel,
        out_shape=(jax.ShapeDtypeStruct((B, Lt, H * HEAD_SLOT), BF16),
                   jax.ShapeDtypeStruct((B, Lt, H * HEAD_SLOT), BF16),
                   jax.ShapeDtypeStruct((B, H * V_DIM, Lt), BF16)),
        grid=(B, nt),
        in_specs=[
            pl.BlockSpec((1, tl, D), lambda b, t: (b, 0, 0)),
            pl.BlockSpec((1, tl, D), lambda b, t: (b, jnp.minimum(t, nt - 2), 0)),
            pl.BlockSpec((1, 6, D), lambda b, t: (b, 0, 0)),
            pl.BlockSpec((1, 6, D), lambda b, t: (0, 0, 0)),
            pl.BlockSpec((1, D), const),
            pl.BlockSpec(wd.shape, const),
            pl.BlockSpec(qn.shape, const),
            pl.BlockSpec(kvn.shape, const),
            pl.BlockSpec(wuq.shape, const),
            pl.BlockSpec(wuk.shape, const),
            pl.BlockSpec(wuv_t.shape, const),
            pl.BlockSpec(gkn.shape, const),
            pl.BlockSpec((tl, 4 * HEAD_SLOT), lambda b, t: (t, 0)),
        ],
        out_specs=(pl.BlockSpec((1, tl, H * HEAD_SLOT), lambda b, t: (b, t, 0)),
                   pl.BlockSpec((1, tl, H * HEAD_SLOT), lambda b, t: (b, t, 0)),
                   pl.BlockSpec((1, H * V_DIM, tl), lambda b, t: (b, 0, t))),
        compiler_params=_cparams(("parallel", "parallel")),
        name="mla_prep",
    )(ctx, x, modl, modc, gain, wd, qn, kvn, wuq, wuk, wuv_t, gkn, tabs)


HEADS_PER_STEP = 2
ATTN_TQ = 2048


ATTN_QCOLS = 256


def _attn_kernel(q_ref, k_ref, vt_ref, o_ref):
    tq = q_ref.shape[1]
    chains = [(qc, hh) for qc in range(tq // ATTN_QCOLS) for hh in range(HEADS_PER_STEP)]
    n = len(chains)

    def scores(qc, hh):
        sl = slice(hh * HEAD_SLOT, (hh + 1) * HEAD_SLOT)
        return lax.dot_general(k_ref[0, :, sl], q_ref[0, qc * ATTN_QCOLS:(qc + 1) * ATTN_QCOLS, sl],
                               (((1,), (1,)), ((), ())), preferred_element_type=F32)

    def softmax(st):
        p = jnp.exp2(st - jnp.max(st, axis=0, keepdims=True))
        return p.astype(BF16), jnp.sum(p, axis=0, keepdims=True)

    def weighted_values(hh, p, l):
        return _dot(vt_ref[0, hh * V_DIM:(hh + 1) * V_DIM, :], p) / l

    st = {0: scores(*chains[0])}
    if n > 1:
        st[1] = scores(*chains[1])
    sm = {0: softmax(st.pop(0))}
    done = {}
    for i in range(n):
        if i + 2 < n:
            st[i + 2] = scores(*chains[i + 2])
        done[chains[i]] = weighted_values(chains[i][1], *sm.pop(i))
        if i + 1 < n:
            sm[i + 1] = softmax(st.pop(i + 1))
        qc, hh = chains[i]
        if hh == HEADS_PER_STEP - 1:
            ot = jnp.concatenate([done.pop((qc, h2)) for h2 in range(HEADS_PER_STEP)], axis=0)
            o_ref[0, qc * ATTN_QCOLS:(qc + 1) * ATTN_QCOLS, :] = ot.T.astype(o_ref.dtype)


def _attention(q, k, vt, q_tile0, n_q, tq, k_tile0, n_k):
    B = q.shape[0]
    H = N_HEADS
    nhp = H // HEADS_PER_STEP
    wq = HEADS_PER_STEP * HEAD_SLOT
    wv = HEADS_PER_STEP * V_DIM
    return pl.pallas_call(
        _attn_kernel,
        out_shape=jax.ShapeDtypeStruct((B, n_q, H * V_DIM), BF16),
        grid=(B, nhp, n_q // tq),
        in_specs=[
            pl.BlockSpec((1, tq, wq), lambda b, h, t: (b, q_tile0 + t, h)),
            pl.BlockSpec((1, n_k, wq), lambda b, h, t: (b, k_tile0, h)),
            pl.BlockSpec((1, wv, n_k), lambda b, h, t: (b, h, k_tile0)),
        ],
        out_specs=pl.BlockSpec((1, tq, wv), lambda b, h, t: (b, t, h)),
        compiler_params=_cparams(("parallel", "parallel", "arbitrary")),
        name="mla_attn",
    )(q, k, vt)


CH_TILE = 256
CONV_PAD = 16
CONV_ROWS = 128
CONF_ROWS = 512


def _conf_a_kernel(x_ref, mod_ref, gain_ref, w1_ref, b1_ref, wdw_ref, bdw_ref, o_ref, h_ref, pad_ref,
                   *, L, R, tc):
    c = pl.program_id(1)
    H = CONV_PAD
    D = x_ref.shape[-1]
    n = L // R

    @pl.when(c == 0)
    def _():
        m = mod_ref[0]
        h_ref[0:H, :] = jnp.zeros((H, D), BF16)
        h_ref[H:H + L, :] = _modulate(x_ref[0], gain_ref[...], m[0:1], m[1:2]).astype(BF16)
        h_ref[H + L:2 * H + L, :] = jnp.zeros((H, D), BF16)

    w = wdw_ref[...]
    bias = bdw_ref[...]
    half = (CONV_W - 1) // 2
    rows = min(CONV_ROWS, R)
    win = rows + 2 * H

    def project(i):
        return _dot(h_ref[i * R:i * R + R + 2 * H, :], w1_ref[0]) + b1_ref[0]

    def glu_conv(i, a):
        pad_ref[i] = a[:, :tc] * _sigmoid(a[:, tc:])
        if i == 0:
            pad_ref[i, 0:H, :] = jnp.zeros((H, tc), F32)
        if i == n - 1:
            pad_ref[i, H + R:2 * H + R, :] = jnp.zeros((H, tc), F32)
        for q0 in range(0, R, rows):
            for l0 in range(0, tc, LANES):
                lanes = slice(l0, l0 + LANES)
                window = pad_ref[i, q0:q0 + win, lanes]
                acc = jnp.broadcast_to(bias[:, lanes], (rows, LANES))
                for r in range(8):
                    rot = window if r == 0 else pltpu.roll(window, win - r, 0)
                    for k in range(CONV_W):
                        off = k + 1 + (H - 1 - half)
                        if off % 8 == r:
                            acc = acc + rot[off - r:off - r + rows, :] * w[k:k + 1, lanes]
                o_ref[0, i * R + q0:i * R + q0 + rows, lanes] = acc

    a = {0: project(0)}
    for i in range(n):
        if i + 1 < n:
            a[i + 1] = project(i + 1)
        glu_conv(i, a.pop(i))


def _conf_b_kernel(u_ref, lng_ref, lnb_ref, w2_ref, b2_ref, x_ref, mod_ref, o_ref):
    u = u_ref[0]
    mu = jnp.mean(u, axis=-1, keepdims=True)
    uc = u - mu
    var = jnp.mean(uc * uc, axis=-1, keepdims=True)
    y = uc * lax.rsqrt(var + EPS) * lng_ref[...] + lnb_ref[...]
    y = _dot(_silu(y).astype(BF16), w2_ref[...]) + b2_ref[...]
    o_ref[0] = x_ref[0] + mod_ref[0, 2:3, :] * y


def _prep_conf(w1, b1, wdw, bdw, lng, lnb, w2, b2):
    D = w1.shape[0]
    tc = CH_TILE
    nc = D // tc
    w1r = w1.reshape(D, 2, nc, tc).transpose(2, 0, 1, 3).reshape(nc, D, 2 * tc).astype(BF16)
    b1r = b1.reshape(2, nc, tc).transpose(1, 0, 2).reshape(nc, 1, 2 * tc)
    return (w1r, b1r, wdw, bdw.reshape(1, D), lng.reshape(1, D), lnb.reshape(1, D), w2.astype(BF16),
            b2.reshape(1, D))


def _conformer(x, mod, gain, cw):
    w1r, b1r, wdw, bdw, lng, lnb, w2, b2 = cw
    B, L, D = x.shape
    tc = CH_TILE
    nc = D // tc
    mod_b = mod.shape[0] > 1
    mod_map2 = (lambda b, c: (b, 0, 0)) if mod_b else (lambda b, c: (0, 0, 0))
    R = min(CONF_ROWS, L)
    u = pl.pallas_call(
        functools.partial(_conf_a_kernel, L=L, R=R, tc=tc),
        out_shape=jax.ShapeDtypeStruct((B, L, D), F32),
        grid=(B, nc),
        in_specs=[
            pl.BlockSpec((1, L, D), lambda b, c: (b, 0, 0)),
            pl.BlockSpec((1, 6, D), mod_map2),
            pl.BlockSpec((1, D), lambda b, c: (0, 0)),
            pl.BlockSpec((1, D, 2 * tc), lambda b, c: (c, 0, 0)),
            pl.BlockSpec((1, 1, 2 * tc), lambda b, c: (c, 0, 0)),
            pl.BlockSpec((CONV_W, tc), lambda b, c: (0, c)),
            pl.BlockSpec((1, tc), lambda b, c: (0, c)),
        ],
        out_specs=pl.BlockSpec((1, L, tc), lambda b, c: (b, 0, c)),
        scratch_shapes=[pltpu.VMEM((L + 2 * CONV_PAD, D), BF16),
                        pltpu.VMEM((L // R, R + 2 * CONV_PAD, tc), F32)],
        compiler_params=_cparams(("parallel", "arbitrary")),
        name="conformer_glu_dwconv",
    )(x, mod, gain, w1r, b1r, wdw, bdw)
    tl = min(512, L)
    const = lambda b, t: (0, 0)
    return pl.pallas_call(
        _conf_b_kernel,
        out_shape=jax.ShapeDtypeStruct((B, L, D), F32),
        grid=(B, L // tl),
        in_specs=[
            pl.BlockSpec((1, tl, D), lambda b, t: (b, t, 0)),
            pl.BlockSpec((1, D), const),
            pl.BlockSpec((1, D), const),
            pl.BlockSpec((D, D), const),
            pl.BlockSpec((1, D), const),
            pl.BlockSpec((1, tl, D), lambda b, t: (b, t, 0)),
            pl.BlockSpec((1, 6, D), (lambda b, t: (b, 0, 0)) if mod_b else (lambda b, t: (0, 0, 0))),
        ],
        out_specs=pl.BlockSpec((1, tl, D), lambda b, t: (b, t, 0)),
        compiler_params=_cparams(("parallel", "parallel")),
        name="conformer_ln_pw2",
    )(u, lng, lnb, w2, b2, x, mod)


@functools.lru_cache(maxsize=None)
def _dft_tables_host(L):
    idx = np.arange(L, dtype=np.int64)
    ang = ((idx[:, None] * idx[None, :]) % (2 * L)).astype(np.float64) * (np.pi / L)
    return np.cos(ang).astype(BF16), np.sin(ang).astype(BF16)


def _dft_tables(L):
    cos_t, sin_t = _dft_tables_host(L)
    return jnp.asarray(cos_t), jnp.asarray(sin_t)


def _filter_features(L):
    t = jnp.linspace(0.0, 1.0, L, dtype=F32)[:, None]
    bands = (POS_EMB - 1) // 2
    w = 2.0 * math.pi * jnp.arange(L, dtype=F32) / L
    f = jnp.linspace(1e-4, bands - 1, bands, dtype=F32)
    fw = w[:, None] * f[None, :]
    z = jnp.concatenate([t, jnp.cos(fw), -jnp.sin(fw)], axis=-1)
    z = jnp.concatenate([z, jnp.zeros((L, LANES - POS_EMB), F32)], axis=-1)
    return t, z.astype(BF16)


def _hy_filter_kernel(z_ref, t_ref, dl_ref, w1_ref, b1_ref, w2_ref, b2_ref, w3f_ref, w3b_ref, fr_ref,
                      cos_ref, sin_ref, kc_ref, ks_ref, kny_ref, *, L):
    fr = fr_ref[...]
    hdn = jnp.sin(fr * (_dot(z_ref[...], w1_ref[...]) + b1_ref[...]))
    hdn = jnp.sin(fr * (_dot(hdn.astype(BF16), w2_ref[...]) + b2_ref[...])).astype(BF16)
    decay = jnp.exp(-t_ref[...] * jnp.abs(dl_ref[...]))
    row = lax.broadcasted_iota(jnp.int32, (L, 1), 0)
    h_fwd = _dot(hdn, w3f_ref[...]) * decay
    h_bwd = jnp.where(row > 0, _dot(hdn, w3b_ref[...]) * decay, 0.0)
    nrm = (jnp.sum(jnp.abs(h_fwd), axis=0, keepdims=True) + jnp.sum(jnp.abs(h_bwd), axis=0, keepdims=True) + EPS)
    inv = 1.0 / nrm
    ksum = (h_fwd + h_bwd) * inv
    kdif = (h_fwd - h_bwd) * inv
    n = 2 * L
    wcol = jnp.where(row == 0, 1.0 / n, 2.0 / n)
    sgn = jnp.where(jnp.bitwise_and(row, 1) == 0, 1.0, -1.0)

    def split_dot(tab, kk):
        hi = kk.astype(BF16)
        lo = (kk - hi.astype(F32)).astype(BF16)
        return _dot(tab, hi) + _dot(tab, lo)

    kc_ref[...] = split_dot(cos_ref[...], ksum) * wcol
    ks_ref[...] = -split_dot(sin_ref[...], kdif) * wcol
    kny_ref[...] = jnp.sum(sgn * ksum, axis=0, keepdims=True) * (1.0 / n)


def _hy_in_kernel(x_ref, mod_ref, gain_ref, win_ref, bin_ref, wsh_ref, bsh_ref, x0_ref, vx_ref, h_ref, pad_ref,
                  *, L, R, tc):
    c = pl.program_id(1)
    H = CONV_PAD
    D = x_ref.shape[-1]
    n = L // R

    @pl.when(c == 0)
    def _():
        m = mod_ref[0]
        h_ref[0:H, :] = jnp.zeros((H, D), BF16)
        h_ref[H:H + L, :] = _modulate(x_ref[0], gain_ref[...], m[0:1], m[1:2]).astype(BF16)
        h_ref[H + L:2 * H + L, :] = jnp.zeros((H, D), BF16)

    w = wsh_ref[0]

    def project(i):
        return _dot(h_ref[i * R:i * R + R + 2 * H, :], win_ref[0]) + bin_ref[0]

    def short_conv(i, a):
        pad_ref[i] = a
        if i == 0:
            pad_ref[i, 0:H, :] = jnp.zeros((H, 3 * tc), F32)
        if i == n - 1:
            pad_ref[i, H + R:2 * H + R, :] = jnp.zeros((H, 3 * tc), F32)
        u = (pad_ref[i, H - 1:H - 1 + R, :] * w[0:1] + pad_ref[i, H:H + R, :] * w[1:2]
             + pad_ref[i, H + 1:H + 1 + R, :] * w[2:3] + bsh_ref[0])
        x0_ref[0, i * R:(i + 1) * R, :] = u[:, :tc]
        vx_ref[0, i * R:(i + 1) * R, :] = u[:, 2 * tc:] * u[:, tc:2 * tc]

    a = {0: project(0)}
    for i in range(n):
        if i + 1 < n:
            a[i + 1] = project(i + 1)
        short_conv(i, a.pop(i))


def _hy_conv_kernel(vx_ref, x0_ref, cr_ref, sr_ref, cc_ref, sc_ref, kc_ref, ks_ref, kny_ref, skip_ref, o_ref,
                    u_ref, acc_ref, *, L):
    f = pl.program_id(2)
    last = pl.num_programs(2) - 1

    @pl.when(f == 0)
    def _():
        u_ref[...] = vx_ref[0].astype(BF16)

    u = u_ref[...]
    tfq = cr_ref.shape[0]
    parts = 2 if tfq % 256 == 0 else 1
    hp = tfq // parts

    def forward(p):
        rows = slice(p * hp, (p + 1) * hp)
        return _dot(cr_ref[rows, :], u), _dot(sr_ref[rows, :], u)

    def spectral(p, pc, ps):
        rows = slice(p * hp, (p + 1) * hp)
        kc = kc_ref[rows, :]
        ks = ks_ref[rows, :]
        return (pc * kc + ps * ks).astype(BF16), (ps * kc - pc * ks).astype(BF16)

    def inverse(p, zc, zs):
        cols = slice(p * hp, (p + 1) * hp)
        return _dot(cc_ref[:, cols], zc) + _dot(sc_ref[:, cols], zs)

    fw = [forward(p) for p in range(parts)]
    y = None
    for p in range(parts):
        yp = inverse(p, *spectral(p, *fw[p]))
        y = yp if y is None else y + yp

    @pl.when(f == 0)
    def _():
        acc_ref[...] = y

    @pl.when(f > 0)
    def _():
        acc_ref[...] += y

    @pl.when(f == last)
    def _():
        vx = vx_ref[0]
        row = lax.broadcasted_iota(jnp.int32, (L, 1), 0)
        sgn = jnp.where(jnp.bitwise_and(row, 1) == 0, 1.0, -1.0)
        u_ny = jnp.sum(sgn * vx, axis=0, keepdims=True)
        yy = acc_ref[...] + sgn * (u_ny * kny_ref[...])
        o_ref[0] = ((yy + skip_ref[...] * vx) * x0_ref[0]).astype(o_ref.dtype)


def _prep_hyena(w_in, b_in, w_short, b_short, f_w1, f_b1, f_w2, f_b2, f_w3, sin_freq, skip, w_out, b_out):
    D = w_in.shape[0]
    tc = CH_TILE
    nc = D // tc
    winr = w_in.reshape(D, 3, nc, tc).transpose(2, 0, 1, 3).reshape(nc, D, 3 * tc).astype(BF16)
    binr = b_in.reshape(3, nc, tc).transpose(1, 0, 2).reshape(nc, 1, 3 * tc)
    wshr = w_short.reshape(3, 3, nc, tc).transpose(2, 0, 1, 3).reshape(nc, 3, 3 * tc)
    bshr = b_short.reshape(3, nc, tc).transpose(1, 0, 2).reshape(nc, 1, 3 * tc)
    fw1 = jnp.concatenate([f_w1, jnp.zeros((LANES - POS_EMB, FILTER_FO), F32)], axis=0).astype(BF16)
    deltas = jnp.linspace(math.log(DECAY_TARGET) / DECAY_FAST, math.log(DECAY_TARGET) / DECAY_SLOW, D,
                          dtype=F32).reshape(1, D)
    return (winr, binr, wshr, bshr, fw1, f_b1.reshape(1, -1), f_w2.astype(BF16), f_b2.reshape(1, -1),
            f_w3.astype(BF16), sin_freq.reshape(1, -1), deltas, skip.reshape(1, D), w_out.astype(BF16),
            b_out.reshape(1, D))


def _hyena_filter(L, hw):
    (_, _, _, _, fw1, fb1, fw2, fb2, fw3, fr, deltas, _, _, _) = hw
    D = deltas.shape[1]
    tc = CH_TILE
    nc = D // tc
    t, z = _filter_features(L)
    cos_t, sin_t = _dft_tables(L)
    const = lambda c: (0, 0)
    kc, ks, kny = pl.pallas_call(
        functools.partial(_hy_filter_kernel, L=L),
        out_shape=(jax.ShapeDtypeStruct((L, D), F32), jax.ShapeDtypeStruct((L, D), F32),
                   jax.ShapeDtypeStruct((1, D), F32)),
        grid=(nc,),
        in_specs=[
            pl.BlockSpec(z.shape, const),
            pl.BlockSpec(t.shape, const),
            pl.BlockSpec((1, tc), lambda c: (0, c)),
            pl.BlockSpec(fw1.shape, const),
            pl.BlockSpec(fb1.shape, const),
            pl.BlockSpec(fw2.shape, const),
            pl.BlockSpec(fb2.shape, const),
            pl.BlockSpec((FILTER_FO, tc), lambda c: (0, c)),
            pl.BlockSpec((FILTER_FO, tc), lambda c: (0, nc + c)),
            pl.BlockSpec(fr.shape, const),
            pl.BlockSpec((L, L), const),
            pl.BlockSpec((L, L), const),
        ],
        out_specs=(pl.BlockSpec((L, tc), lambda c: (0, c)), pl.BlockSpec((L, tc), lambda c: (0, c)),
                   pl.BlockSpec((1, tc), lambda c: (0, c))),
        compiler_params=_cparams(("arbitrary",), VMEM_BIG),
        name="hyena_filter",
    )(z, t, deltas, fw1, fb1, fw2, fb2, fw3, fw3, fr, cos_t, sin_t)
    return cos_t, sin_t, kc, ks, kny


LCONV_TC = 512
LCONV_TF = 512


def _hyena(x, mod, gain, hw, filt):
    (winr, binr, wshr, bshr, _, _, _, _, _, _, _, skip, w_out, b_out) = hw
    cos_t, sin_t, kc, ks, kny = filt
    B, L, D = x.shape
    tc = CH_TILE
    nc = D // tc
    mod_b = mod.shape[0] > 1
    R = min(CONF_ROWS, L)
    x0, vx = pl.pallas_call(
        functools.partial(_hy_in_kernel, L=L, R=R, tc=tc),
        out_shape=(jax.ShapeDtypeStruct((B, L, D), F32), jax.ShapeDtypeStruct((B, L, D), F32)),
        grid=(B, nc),
        in_specs=[
            pl.BlockSpec((1, L, D), lambda b, c: (b, 0, 0)),
            pl.BlockSpec((1, 6, D), (lambda b, c: (b, 0, 0)) if mod_b else (lambda b, c: (0, 0, 0))),
            pl.BlockSpec((1, D), lambda b, c: (0, 0)),
            pl.BlockSpec((1, D, 3 * tc), lambda b, c: (c, 0, 0)),
            pl.BlockSpec((1, 1, 3 * tc), lambda b, c: (c, 0, 0)),
            pl.BlockSpec((1, 3, 3 * tc), lambda b, c: (c, 0, 0)),
            pl.BlockSpec((1, 1, 3 * tc), lambda b, c: (c, 0, 0)),
        ],
        out_specs=(pl.BlockSpec((1, L, tc), lambda b, c: (b, 0, c)), pl.BlockSpec((1, L, tc), lambda b, c: (b, 0, c))),
        scratch_shapes=[pltpu.VMEM((L + 2 * CONV_PAD, D), BF16),
                        pltpu.VMEM((L // R, R + 2 * CONV_PAD, 3 * tc), F32)],
        compiler_params=_cparams(("parallel", "arbitrary"), VMEM_BIG),
        name="hyena_in_shortconv",
    )(x, mod, gain, winr, binr, wshr, bshr)

    tcl = min(LCONV_TC, D)
    tfq = min(LCONV_TF, L)
    y = pl.pallas_call(
        functools.partial(_hy_conv_kernel, L=L),
        out_shape=jax.ShapeDtypeStruct((B, L, D), BF16),
        grid=(B, D // tcl, L // tfq),
        in_specs=[
            pl.BlockSpec((1, L, tcl), lambda b, c, f: (b, 0, c)),
            pl.BlockSpec((1, L, tcl), lambda b, c, f: (b, 0, c)),
            pl.BlockSpec((tfq, L), lambda b, c, f: (f, 0)),
            pl.BlockSpec((tfq, L), lambda b, c, f: (f, 0)),
            pl.BlockSpec((L, tfq), lambda b, c, f: (0, f)),
            pl.BlockSpec((L, tfq), lambda b, c, f: (0, f)),
            pl.BlockSpec((tfq, tcl), lambda b, c, f: (f, c)),
            pl.BlockSpec((tfq, tcl), lambda b, c, f: (f, c)),
            pl.BlockSpec((1, tcl), lambda b, c, f: (0, c)),
            pl.BlockSpec((1, tcl), lambda b, c, f: (0, c)),
        ],
        out_specs=pl.BlockSpec((1, L, tcl), lambda b, c, f: (b, 0, c)),
        scratch_shapes=[pltpu.VMEM((L, tcl), BF16), pltpu.VMEM((L, tcl), F32)],
        compiler_params=_cparams(("parallel", "parallel", "arbitrary"), VMEM_BIG),
        name="hyena_longconv",
    )(vx, x0, cos_t, sin_t, cos_t, sin_t, kc, ks, kny, skip)
    return _proj_resid(y, w_out, b_out, x, mod, 2, 512)


def kernel(x, c, ctx, c_ctx, ada_w, ada_b, norm_mix, norm_ffn, mla_w_dq, mla_q_norm, mla_w_uq, mla_w_dkv, mla_kv_norm, mla_w_ukv, mla_qk_gain, mla_w_o, cf_w_pw1, cf_b_pw1, cf_w_dw, cf_b_dw, cf_ln_g, cf_ln_b, cf_w_pw2, cf_b_pw2, hy_w_in, hy_b_in, hy_w_short, hy_b_short, hy_f_w1, hy_f_b1, hy_f_w2, hy_f_b2, hy_f_w3, hy_sin_freq, hy_skip, hy_w_out, hy_b_out, ffn_w_up, ffn_w_dw, ffn_b_dw, ffn_w_down):
    B, L, D = x.shape
    Lc = ctx.shape[1]
    depth = ada_w.shape[0]

    rows = ((B + 1 + 7) // 8) * 8
    cvec = jnp.concatenate([c, c_ctx[None, :], jnp.zeros((rows - B - 1, D), F32)], axis=0)
    ada = _ada_all(cvec, ada_w, ada_b)
    zero_bias = jnp.zeros((1, D), F32)
    rope = None
    fw = _prep_ffn(ffn_w_up, ffn_w_dw, ffn_b_dw, ffn_w_down)

    for i in range(depth):
        kind = i % N_MIXERS
        j = i // N_MIXERS
        need_ctx_out = i < depth - 1
        modl = ada[i, :B].reshape(B, 6, D)
        modc = ada[i, B:B + 1].reshape(1, 6, D)
        gmix = norm_mix[i].reshape(1, D)
        gffn = norm_ffn[i].reshape(1, D)
        xc = None
        if kind == 0:
            mw = _prep_mla(mla_w_dq[j], mla_q_norm[j], mla_w_uq[j], mla_w_dkv[j], mla_kv_norm[j], mla_w_ukv[j],
                           mla_qk_gain[j], mla_w_o[j])
            if rope is None:
                rope = _rope_table(Lc, L)
            q, k, vt = _mla_prep(ctx, x, modl, modc, gmix, mw, rope)
            o = _attention(q, k, vt, 0, L, min(ATTN_TQ, L), 0, Lc + L)
            x_new = _proj_resid(o, mw[-1], zero_bias, x, modl, 2, 512)
            if need_ctx_out:
                oc = _attention(q, k, vt, L // Lc, Lc, Lc, L // Lc, Lc)
                xc = _proj_resid(oc, mw[-1], zero_bias, ctx, modc, 2, 512)
            x = x_new
        elif kind == 1:
            cw = _prep_conf(cf_w_pw1[j], cf_b_pw1[j], cf_w_dw[j], cf_b_dw[j], cf_ln_g[j], cf_ln_b[j], cf_w_pw2[j],
                            cf_b_pw2[j])
            x = _conformer(x, modl, gmix, cw)
            if need_ctx_out:
                xc = _conformer(ctx, modc, gmix, cw)
        else:
            hw = _prep_hyena(hy_w_in[j], hy_b_in[j], hy_w_short[j], hy_b_short[j], hy_f_w1[j], hy_f_b1[j],
                             hy_f_w2[j], hy_f_b2[j], hy_f_w3[j], hy_sin_freq[j], hy_skip[j], hy_w_out[j],
                             hy_b_out[j])
            x = _hyena(x, modl, gmix, hw, _hyena_filter(L, hw))
            if need_ctx_out:
                xc = _hyena(ctx, modc, gmix, hw, _hyena_filter(Lc, hw))
        x = _ffn(x, modl, gffn, fw, i)
        if need_ctx_out:
            ctx = _ffn(xc, modc, gffn, fw, i)
    return x
```

```python
import functools
import math

import numpy as np
import jax
import jax.numpy as jnp
from jax import lax
from jax.experimental import pallas as pl
from jax.experimental.pallas import tpu as pltpu

F32 = jnp.float32
BF16 = jnp.bfloat16

D_MODEL = 1024
DEPTH = 4
GRID_W = 64
N_MIXERS = 3
EPS = 1e-6
N_HEADS = 16
QK_NOPE = 64
QK_ROPE = 32
V_DIM = 64
Q_LORA = 256
KV_LORA = 128
ROPE_THETA = 10000.0
CONV_W = 31
POS_EMB = 33
FILTER_FO = 64
DECAY_FAST = 0.3
DECAY_SLOW = 1.5
DECAY_TARGET = 1e-2
D_FF = 2816

LANES = 128
HEAD_SLOT = 128
MIB = 1024 * 1024
VMEM_BIG = 58 * MIB
VMEM_MID = 48 * MIB


def _cparams(sem, vmem=VMEM_MID):
    return pltpu.CompilerParams(dimension_semantics=sem, vmem_limit_bytes=vmem)


def _sigmoid(x):
    return 1.0 / (1.0 + jnp.exp(-x))


def _silu(x):
    return x * _sigmoid(x)


def _rms_scale(x, n):
    return lax.rsqrt(jnp.sum(x * x, axis=-1, keepdims=True) * (1.0 / n) + EPS)


def _modulate(x, gain, shift, scale):
    y = x * _rms_scale(x, x.shape[-1]) * gain
    return y * (1.0 + scale) + shift


def _dot(a, b):
    return jnp.dot(a, b, preferred_element_type=F32)


def _ada_kernel(c_ref, w_ref, b_ref, o_ref):
    s = _silu(c_ref[...]).astype(BF16)
    o_ref[0] = _dot(s, w_ref[0].astype(BF16)) + b_ref[0]


def _ada_all(cvec, ada_w, ada_b):
    rows = cvec.shape[0]
    depth, d, n = ada_w.shape
    tn = 1536
    return pl.pallas_call(
        _ada_kernel,
        out_shape=jax.ShapeDtypeStruct((depth, rows, n), F32),
        grid=(depth, n // tn),
        in_specs=[
            pl.BlockSpec((rows, d), lambda i, j: (0, 0)),
            pl.BlockSpec((1, d, tn), lambda i, j: (i, 0, j)),
            pl.BlockSpec((1, 1, tn), lambda i, j: (i, 0, j)),
        ],
        out_specs=pl.BlockSpec((1, rows, tn), lambda i, j: (i, 0, j)),
        compiler_params=_cparams(("parallel", "parallel")),
        name="ada_mod",
    )(cvec, ada_w, ada_b.reshape(depth, 1, n))


def _proj_resid_kernel(a_ref, w_ref, b_ref, x_ref, mod_ref, o_ref, *, gate_row):
    y = _dot(a_ref[0], w_ref[...]) + b_ref[...]
    o_ref[0] = x_ref[0] + mod_ref[0, gate_row:gate_row + 1, :] * y


def _proj_resid(a, w, bias, x, mod, gate_row, tl):
    B, L, K = a.shape
    D = w.shape[1]
    tl = min(tl, L)
    mod_b = mod.shape[0] > 1
    return pl.pallas_call(
        functools.partial(_proj_resid_kernel, gate_row=gate_row),
        out_shape=jax.ShapeDtypeStruct((B, L, D), F32),
        grid=(B, L // tl),
        in_specs=[
            pl.BlockSpec((1, tl, K), lambda b, t: (b, t, 0)),
            pl.BlockSpec((K, D), lambda b, t: (0, 0)),
            pl.BlockSpec((1, D), lambda b, t: (0, 0)),
            pl.BlockSpec((1, tl, D), lambda b, t: (b, t, 0)),
            pl.BlockSpec((1, 6, D), (lambda b, t: (b, 0, 0)) if mod_b else (lambda b, t: (0, 0, 0))),
        ],
        out_specs=pl.BlockSpec((1, tl, D), lambda b, t: (b, t, 0)),
        compiler_params=_cparams(("parallel", "parallel")),
        name="proj_resid",
    )(a, w, bias, x, mod)


FFN_TILE = 256


FFN_HALO = 16
FFN_ROWS = 512


def _ffn_kernel(xp_ref, x_ref, xn_ref, mod_ref, gain_ref, wup_ref, wdw_ref, bdw_ref, wdn_ref, o_ref,
                h_ref, pad_ref, act_ref, *, R, tf, whole_seq):
    c = pl.program_id(1)
    nc = pl.num_programs(1)
    H = FFN_HALO
    D = x_ref.shape[-1]
    nj = D_FF // tf
    m = mod_ref[0]
    gain = gain_ref[...]

    def modulated(v):
        return _modulate(v, gain, m[3:4], m[4:5])

    h_ref[H:H + R, :] = modulated(x_ref[0]).astype(BF16)
    if whole_seq:
        h_ref[0:H, :] = jnp.zeros((H, D), BF16)
        h_ref[H + R:2 * H + R, :] = jnp.zeros((H, D), BF16)
    else:
        h_ref[0:H, :] = jnp.where(c > 0, modulated(xp_ref[0]), 0.0).astype(BF16)
        h_ref[H + R:2 * H + R, :] = jnp.where(c < nc - 1, modulated(xn_ref[0]), 0.0).astype(BF16)

    def up(t):
        cols = slice(t * tf, (t + 1) * tf)
        pad_ref[t] = _dot(h_ref[...], wup_ref[0, :, cols])
        return _dot(h_ref[H:H + R, :], wup_ref[0, :, D_FF + t * tf:D_FF + (t + 1) * tf])

    def gate(t, lin):
        cols = slice(t * tf, (t + 1) * tf)
        w = wdw_ref[0, :, cols]
        g = (pad_ref[t, H - 1:H - 1 + R, :] * w[0:1] + pad_ref[t, H:H + R, :] * w[1:2]
             + pad_ref[t, H + 1:H + 1 + R, :] * w[2:3] + bdw_ref[0, :, cols])
        act_ref[:, cols] = (_silu(g) * lin).astype(BF16)

    lin = {0: up(0)}
    for t in range(nj):
        if t + 1 < nj:
            lin[t + 1] = up(t + 1)
        gate(t, lin.pop(t))
    y = _dot(act_ref[...], wdn_ref[0])
    o_ref[0] = x_ref[0] + m[5:6] * y


def _prep_ffn(w_up, w_dw, b_dw, w_down):
    depth = w_up.shape[0]
    return w_up.astype(BF16), w_dw, b_dw.reshape(depth, 1, D_FF), w_down.astype(BF16)


def _ffn(x, mod, gain, ffn_w, layer):
    wup, wdw, bdw, wdn = ffn_w
    B, L, D = x.shape
    tf = FFN_TILE
    nj = D_FF // tf
    mod_b = mod.shape[0] > 1
    R = min(FFN_ROWS, L)
    H = FFN_HALO
    hb = R // H
    last_hb = L // H - 1
    resident = pl.Buffered(1)
    return pl.pallas_call(
        functools.partial(_ffn_kernel, R=R, tf=tf, whole_seq=(R == L)),
        out_shape=jax.ShapeDtypeStruct((B, L, D), F32),
        grid=(B, L // R),
        in_specs=[
            pl.BlockSpec((1, H, D), lambda b, c: (b, jnp.maximum(c * hb - 1, 0), 0)),
            pl.BlockSpec((1, R, D), lambda b, c: (b, c, 0)),
            pl.BlockSpec((1, H, D), lambda b, c: (b, jnp.minimum((c + 1) * hb, last_hb), 0)),
            pl.BlockSpec((1, 6, D), (lambda b, c: (b, 0, 0)) if mod_b else (lambda b, c: (0, 0, 0))),
            pl.BlockSpec((1, D), lambda b, c: (0, 0)),
            pl.BlockSpec((1, D, 2 * D_FF), lambda b, c: (layer, 0, 0), pipeline_mode=resident),
            pl.BlockSpec((1, 3, D_FF), lambda b, c: (layer, 0, 0)),
            pl.BlockSpec((1, 1, D_FF), lambda b, c: (layer, 0, 0)),
            pl.BlockSpec((1, D_FF, D), lambda b, c: (layer, 0, 0), pipeline_mode=resident),
        ],
        out_specs=pl.BlockSpec((1, R, D), lambda b, c: (b, c, 0)),
        scratch_shapes=[pltpu.VMEM((R + 2 * H, D), BF16),
                        pltpu.VMEM((nj, R + 2 * H, tf), F32),
                        pltpu.VMEM((R, D_FF), BF16)],
        compiler_params=_cparams(("parallel", "parallel"), VMEM_BIG),
        name="conv_ffn",
    )(x, x, x, mod, gain, wup, wdw, bdw, wdn)


MLA_TL = 256
MLA_SUB = 128


def _rope_partner(j):
    return j + 8 if (j % 16) < 8 else j - 8


def _rope_table(n_ctx, L):
    rows = L // GRID_W
    row = jnp.repeat(jnp.arange(rows), GRID_W).astype(F32)
    col = jnp.tile(jnp.arange(GRID_W), rows).astype(F32)
    half = QK_ROPE // 2
    inv = ROPE_THETA ** (-(jnp.arange(0, half, 2, dtype=F32) / half))
    ang = jnp.concatenate([row[:, None] * inv, col[:, None] * inv], axis=-1)
    ang = jnp.concatenate([ang, jnp.zeros((n_ctx, QK_ROPE // 2), F32)], axis=0)
    idx = jnp.array([(j // 16) * 8 + (j % 8) for j in range(QK_ROPE)])
    sgn = jnp.array([-1.0 if (j % 16) < 8 else 1.0 for j in range(QK_ROPE)], F32)
    a = ang[:, idx]
    ones = jnp.ones((ang.shape[0], QK_NOPE), F32)
    return jnp.concatenate([ones, jnp.cos(a), jnp.sin(a) * sgn], axis=-1)


def _prep_mla(w_dq, q_norm, w_uq, w_dkv, kv_norm, w_ukv, qk_gain, w_o):
    H = N_HEADS
    perm = jnp.array([_rope_partner(j) for j in range(QK_ROPE)])
    D = w_dq.shape[0]
    w_kpe = w_dkv[:, KV_LORA:]
    wd = jnp.concatenate([w_dq, w_dkv[:, :KV_LORA], jnp.zeros((D, QK_NOPE), F32), w_kpe, w_kpe[:, perm]],
                         axis=1).astype(BF16)
    wq = w_uq.reshape(Q_LORA, H, QK_NOPE + QK_ROPE)
    wq_pe = wq[:, :, QK_NOPE:]
    wuq = jnp.concatenate([wq, wq_pe[:, :, perm]], axis=-1).reshape(Q_LORA, H * HEAD_SLOT).astype(BF16)
    wkv = w_ukv.reshape(KV_LORA, H, QK_NOPE + V_DIM)
    wk = jnp.concatenate([wkv[:, :, :QK_NOPE], jnp.zeros((KV_LORA, H, HEAD_SLOT - QK_NOPE), F32)], axis=-1)
    wuk = wk.reshape(KV_LORA, H * HEAD_SLOT).astype(BF16)
    wuv_t = wkv[:, :, QK_NOPE:].reshape(KV_LORA, H * V_DIM).T.astype(BF16)
    scale = (QK_NOPE + QK_ROPE) ** -0.5 * math.log2(math.e)
    cn, cp = math.sqrt(QK_NOPE), math.sqrt(QK_ROPE)
    gq = qk_gain[0]
    gq_slot = jnp.concatenate([gq[:QK_NOPE] * cn, gq[QK_NOPE:] * cp, gq[QK_NOPE:][perm] * cp]) * scale
    gk = qk_gain[1]
    gk_pe_slot = jnp.concatenate([jnp.zeros((QK_NOPE,), F32), gk[QK_NOPE:], gk[QK_NOPE:][perm]]) * cp
    gk_n_slot = jnp.concatenate([gk[:QK_NOPE] * cn, jnp.zeros((HEAD_SLOT - QK_NOPE,), F32)])
    gains = jnp.stack([gq_slot, gk_pe_slot, gk_n_slot], axis=0)
    return (wd, q_norm.reshape(1, Q_LORA), kv_norm.reshape(1, KV_LORA), wuq, wuk, wuv_t, gains, w_o.astype(BF16))


def _mla_prep_kernel(ctx_ref, x_ref, modl_ref, modc_ref, gain_ref, wd_ref, qn_ref, kvn_ref, wuq_ref, wuk_ref,
                     wuvt_ref, gkn_ref, tab_ref, q_ref, k_ref, vt_ref):
    is_ctx = pl.program_id(1) == pl.num_programs(1) - 1
    m = jnp.where(is_ctx, modc_ref[0], modl_ref[0])
    lane = lax.broadcasted_iota(jnp.int32, (1, HEAD_SLOT), 1)
    nope_mask = lane < QK_NOPE
    pe_mask = jnp.logical_and(lane >= QK_NOPE, lane < QK_NOPE + QK_ROPE)
    g_kn = gkn_ref[...]
    fold = HEAD_SLOT - QK_ROPE
    tl = x_ref.shape[1]
    parts = tl // MLA_SUB

    def sumsq(v):
        return jnp.sum(v * v, axis=-1, keepdims=True)

    def project(p):
        rs = slice(p * MLA_SUB, (p + 1) * MLA_SUB)
        xin = jnp.where(is_ctx, ctx_ref[0, rs, :], x_ref[0, rs, :])
        h = _modulate(xin, gain_ref[...], m[0:1], m[1:2]).astype(BF16)
        d = _dot(h, wd_ref[...])
        cq = d[:, :Q_LORA]
        cq = (cq * _rms_scale(cq, Q_LORA) * qn_ref[...]).astype(BF16)
        ckv = d[:, Q_LORA:Q_LORA + KV_LORA]
        ckv = (ckv * _rms_scale(ckv, KV_LORA) * kvn_ref[...]).astype(BF16)
        kp = d[:, Q_LORA + KV_LORA:]
        xk = kp * lax.rsqrt(0.5 * sumsq(kp) + QK_ROPE * EPS)
        kpe_slot = (xk * tab_ref[rs, 2 * HEAD_SLOT:3 * HEAD_SLOT]
                    + pltpu.roll(xk * tab_ref[rs, 3 * HEAD_SLOT:4 * HEAD_SLOT], fold, 1))
        qraw = _dot(cq, wuq_ref[...])
        kvu = _dot(ckv, wuk_ref[...])
        vt_ref[0, :, rs] = lax.dot_general(wuvt_ref[...], ckv, (((1,), (1,)), ((), ())),
                                           preferred_element_type=F32).astype(BF16)
        return qraw, kvu, kpe_slot

    def heads(p, qraw, kvu, kpe_slot):
        rs = slice(p * MLA_SUB, (p + 1) * MLA_SUB)
        tq_a = tab_ref[rs, 0 * HEAD_SLOT:1 * HEAD_SLOT]
        tq_b = tab_ref[rs, 1 * HEAD_SLOT:2 * HEAD_SLOT]

        def norms(hd):
            sl = slice(hd * HEAD_SLOT, (hd + 1) * HEAD_SLOT)
            sq = qraw[:, sl] * qraw[:, sl]
            rn = lax.rsqrt(jnp.sum(jnp.where(nope_mask, sq, 0.0), axis=-1, keepdims=True) + QK_NOPE * EPS)
            rp = lax.rsqrt(jnp.sum(jnp.where(pe_mask, sq, 0.0), axis=-1, keepdims=True) + QK_ROPE * EPS)
            rk = lax.rsqrt(sumsq(kvu[:, sl]) + QK_NOPE * EPS)
            return rn, rp, rk

        def emit(hd, rn, rp, rk):
            sl = slice(hd * HEAD_SLOT, (hd + 1) * HEAD_SLOT)
            xq = qraw[:, sl] * jnp.where(nope_mask, rn, rp)
            q_ref[0, rs, sl] = (xq * tq_a + pltpu.roll(xq * tq_b, fold, 1)).astype(BF16)
            k_ref[0, rs, sl] = (kvu[:, sl] * rk * g_kn + kpe_slot).astype(BF16)

        ahead = 2
        stats = {hd: norms(hd) for hd in range(min(ahead, N_HEADS))}
        for hd in range(N_HEADS):
            if hd + ahead < N_HEADS:
                stats[hd + ahead] = norms(hd + ahead)
            emit(hd, *stats.pop(hd))

    proj = {0: project(0)}
    for p in range(parts):
        if p + 1 < parts:
            proj[p + 1] = project(p + 1)
        heads(p, *proj.pop(p))


def _mla_prep(ctx, x, modl, modc, gain, mw, rope):
    wd, qn, kvn, wuq, wuk, wuv_t, gains, _ = mw
    B, L, D = x.shape
    Lc = ctx.shape[1]
    tl = MLA_TL
    assert Lc == tl and L % tl == 0
    nt = (Lc + L) // tl
    Lt = Lc + L
    H = N_HEADS
    const = lambda b, t: (0, 0)
    partner_lanes = jnp.arange(HEAD_SLOT) >= QK_NOPE + QK_ROPE
    tq = rope * gains[0:1]
    tk = rope * gains[1:2]
    tabs = jnp.concatenate([jnp.where(partner_lanes, 0.0, tq), jnp.where(partner_lanes, tq, 0.0),
                            jnp.where(partner_lanes, 0.0, tk), jnp.where(partner_lanes, tk, 0.0)], axis=1)
    gkn = gains[2:3]
    return pl.pallas_call(
        _mla_prep_kernel,
        out_shape=(jax.ShapeDtypeStruct((B, Lt, H * HEAD_SLOT), BF16),
                   jax.ShapeDtypeStruct((B, Lt, H * HEAD_SLOT), BF16),
                   jax.ShapeDtypeStruct((B, H * V_DIM, Lt), BF16)),
        grid=(B, nt),
        in_specs=[
            pl.BlockSpec((1, tl, D), lambda b, t: (b, 0, 0)),
            pl.BlockSpec((1, tl, D), lambda b, t: (b, jnp.minimum(t, nt - 2), 0)),
            pl.BlockSpec((1, 6, D), lambda b, t: (b, 0, 0)),
            pl.BlockSpec((1, 6, D), lambda b, t: (0, 0, 0)),
            pl.BlockSpec((1, D), const),
            pl.BlockSpec(wd.shape, const),
            pl.BlockSpec(qn.shape, const),
            pl.BlockSpec(kvn.shape, const),
            pl.BlockSpec(wuq.shape, const),
            pl.BlockSpec(wuk.shape, const),
            pl.BlockSpec(wuv_t.shape, const),
            pl.BlockSpec(gkn.shape, const),
            pl.BlockSpec((tl, 4 * HEAD_SLOT), lambda b, t: (t, 0)),
        ],
        out_specs=(pl.BlockSpec((1, tl, H * HEAD_SLOT), lambda b, t: (b, t, 0)),
                   pl.BlockSpec((1, tl, H * HEAD_SLOT), lambda b, t: (b, t, 0)),
                   pl.BlockSpec((1, H * V_DIM, tl), lambda b, t: (b, 0, t))),
        compiler_params=_cparams(("parallel", "parallel")),
        name="mla_prep",
    )(ctx, x, modl, modc, gain, wd, qn, kvn, wuq, wuk, wuv_t, gkn, tabs)


HEADS_PER_STEP = 2
ATTN_TQ = 2048


ATTN_QCOLS = 256
ATTN_KEY_BLOCK = 128


def _attn_kernel(q_ref, k_ref, vt_ref, o_ref, st_ref, p_ref):
    tq = q_ref.shape[1]
    chains = [(qc, hh) for qc in range(tq // ATTN_QCOLS) for hh in range(HEADS_PER_STEP)]
    n = len(chains)

    nk = k_ref.shape[1]
    s_slots = st_ref.shape[0]
    p_slots = p_ref.shape[0]
    kb = ATTN_KEY_BLOCK
    grp = (kb // 8, 8, ATTN_QCOLS)

    def scores(i):
        qc, hh = chains[i]
        sl = slice(hh * HEAD_SLOT, (hh + 1) * HEAD_SLOT)
        st_ref[i % s_slots] = lax.dot_general(
            k_ref[0, :, sl], q_ref[0, qc * ATTN_QCOLS:(qc + 1) * ATTN_QCOLS, sl],
            (((1,), (1,)), ((), ())), preferred_element_type=F32)

    def softmax(i):
        s_slot, p_slot = i % s_slots, i % p_slots
        mx = jnp.full(grp[1:], -jnp.inf, F32)
        for r0 in range(0, nk, kb):
            mx = jnp.maximum(mx, jnp.max(st_ref[s_slot, r0:r0 + kb, :].reshape(grp), axis=0))
        m = jnp.max(mx, axis=0, keepdims=True)
        acc = jnp.zeros(grp[1:], F32)
        for r0 in range(0, nk, kb):
            p = jnp.exp2(st_ref[s_slot, r0:r0 + kb, :] - m)
            acc = acc + jnp.sum(p.reshape(grp), axis=0)
            p_ref[p_slot, r0:r0 + kb, :] = p.astype(BF16)
        return jnp.sum(acc, axis=0, keepdims=True)

    def weighted_values(i, l):
        hh = chains[i][1]
        return _dot(vt_ref[0, hh * V_DIM:(hh + 1) * V_DIM, :], p_ref[i % p_slots]) / l

    scores(0)
    if n > 1:
        scores(1)
    sm = {0: softmax(0)}
    done = {}
    for i in range(n):
        if i + 2 < n:
            scores(i + 2)
        done[chains[i]] = weighted_values(i, sm.pop(i))
        if i + 1 < n:
            sm[i + 1] = softmax(i + 1)
        qc, hh = chains[i]
        if hh == HEADS_PER_STEP - 1:
            ot = jnp.concatenate([done.pop((qc, h2)) for h2 in range(HEADS_PER_STEP)], axis=0)
            o_ref[0, qc * ATTN_QCOLS:(qc + 1) * ATTN_QCOLS, :] = ot.T.astype(o_ref.dtype)


def _attention(q, k, vt, q_tile0, n_q, tq, k_tile0, n_k):
    B = q.shape[0]
    H = N_HEADS
    nhp = H // HEADS_PER_STEP
    wq = HEADS_PER_STEP * HEAD_SLOT
    wv = HEADS_PER_STEP * V_DIM
    return pl.pallas_call(
        _attn_kernel,
        out_shape=jax.ShapeDtypeStruct((B, n_q, H * V_DIM), BF16),
        grid=(B, nhp, n_q // tq),
        in_specs=[
            pl.BlockSpec((1, tq, wq), lambda b, h, t: (b, q_tile0 + t, h)),
            pl.BlockSpec((1, n_k, wq), lambda b, h, t: (b, k_tile0, h)),
            pl.BlockSpec((1, wv, n_k), lambda b, h, t: (b, h, k_tile0)),
        ],
        out_specs=pl.BlockSpec((1, tq, wv), lambda b, h, t: (b, t, h)),
        scratch_shapes=[pltpu.VMEM((3, n_k, ATTN_QCOLS), F32), pltpu.VMEM((2, n_k, ATTN_QCOLS), BF16)],
        compiler_params=_cparams(("parallel", "parallel", "arbitrary")),
        name="mla_attn",
    )(q, k, vt)


CH_TILE = 256
CONV_PAD = 16
CONV_ROWS = 128
CONF_ROWS = 512


def _conf_a_kernel(x_ref, mod_ref, gain_ref, w1_ref, b1_ref, wdw_ref, bdw_ref, o_ref, h_ref, pad_ref,
                   *, L, R, tc):
    c = pl.program_id(1)
    H = CONV_PAD
    D = x_ref.shape[-1]
    n = L // R

    @pl.when(c == 0)
    def _():
        m = mod_ref[0]
        h_ref[0:H, :] = jnp.zeros((H, D), BF16)
        h_ref[H:H + L, :] = _modulate(x_ref[0], gain_ref[...], m[0:1], m[1:2]).astype(BF16)
        h_ref[H + L:2 * H + L, :] = jnp.zeros((H, D), BF16)

    w = wdw_ref[...]
    bias = bdw_ref[...]
    half = (CONV_W - 1) // 2
    rows = min(CONV_ROWS, R)
    win = rows + 2 * H

    def project(i):
        return _dot(h_ref[i * R:i * R + R + 2 * H, :], w1_ref[0]) + b1_ref[0]

    def glu_conv(i, a):
        pad_ref[i] = a[:, :tc] * _sigmoid(a[:, tc:])
        if i == 0:
            pad_ref[i, 0:H, :] = jnp.zeros((H, tc), F32)
        if i == n - 1:
            pad_ref[i, H + R:2 * H + R, :] = jnp.zeros((H, tc), F32)
        for q0 in range(0, R, rows):
            for l0 in range(0, tc, LANES):
                lanes = slice(l0, l0 + LANES)
                window = pad_ref[i, q0:q0 + win, lanes]
                acc = jnp.broadcast_to(bias[:, lanes], (rows, LANES))
                for r in range(8):
                    rot = window if r == 0 else pltpu.roll(window, win - r, 0)
                    for k in range(CONV_W):
                        off = k + 1 + (H - 1 - half)
                        if off % 8 == r:
                            acc = acc + rot[off - r:off - r + rows, :] * w[k:k + 1, lanes]
                o_ref[0, i * R + q0:i * R + q0 + rows, lanes] = acc

    a = {0: project(0)}
    for i in range(n):
        if i + 1 < n:
            a[i + 1] = project(i + 1)
        glu_conv(i, a.pop(i))


def _conf_b_kernel(u_ref, lng_ref, lnb_ref, w2_ref, b2_ref, x_ref, mod_ref, o_ref):
    u = u_ref[0]
    mu = jnp.mean(u, axis=-1, keepdims=True)
    uc = u - mu
    var = jnp.mean(uc * uc, axis=-1, keepdims=True)
    y = uc * lax.rsqrt(var + EPS) * lng_ref[...] + lnb_ref[...]
    y = _dot(_silu(y).astype(BF16), w2_ref[...]) + b2_ref[...]
    o_ref[0] = x_ref[0] + mod_ref[0, 2:3, :] * y


def _prep_conf(w1, b1, wdw, bdw, lng, lnb, w2, b2):
    D = w1.shape[0]
    tc = CH_TILE
    nc = D // tc
    w1r = w1.reshape(D, 2, nc, tc).transpose(2, 0, 1, 3).reshape(nc, D, 2 * tc).astype(BF16)
    b1r = b1.reshape(2, nc, tc).transpose(1, 0, 2).reshape(nc, 1, 2 * tc)
    return (w1r, b1r, wdw, bdw.reshape(1, D), lng.reshape(1, D), lnb.reshape(1, D), w2.astype(BF16),
            b2.reshape(1, D))


def _conformer(x, mod, gain, cw):
    w1r, b1r, wdw, bdw, lng, lnb, w2, b2 = cw
    B, L, D = x.shape
    tc = CH_TILE
    nc = D // tc
    mod_b = mod.shape[0] > 1
    mod_map2 = (lambda b, c: (b, 0, 0)) if mod_b else (lambda b, c: (0, 0, 0))
    R = min(CONF_ROWS, L)
    u = pl.pallas_call(
        functools.partial(_conf_a_kernel, L=L, R=R, tc=tc),
        out_shape=jax.ShapeDtypeStruct((B, L, D), F32),
        grid=(B, nc),
        in_specs=[
            pl.BlockSpec((1, L, D), lambda b, c: (b, 0, 0)),
            pl.BlockSpec((1, 6, D), mod_map2),
            pl.BlockSpec((1, D), lambda b, c: (0, 0)),
            pl.BlockSpec((1, D, 2 * tc), lambda b, c: (c, 0, 0)),
            pl.BlockSpec((1, 1, 2 * tc), lambda b, c: (c, 0, 0)),
            pl.BlockSpec((CONV_W, tc), lambda b, c: (0, c)),
            pl.BlockSpec((1, tc), lambda b, c: (0, c)),
        ],
        out_specs=pl.BlockSpec((1, L, tc), lambda b, c: (b, 0, c)),
        scratch_shapes=[pltpu.VMEM((L + 2 * CONV_PAD, D), BF16),
                        pltpu.VMEM((L // R, R + 2 * CONV_PAD, tc), F32)],
        compiler_params=_cparams(("parallel", "arbitrary")),
        name="conformer_glu_dwconv",
    )(x, mod, gain, w1r, b1r, wdw, bdw)
    tl = min(512, L)
    const = lambda b, t: (0, 0)
    return pl.pallas_call(
        _conf_b_kernel,
        out_shape=jax.ShapeDtypeStruct((B, L, D), F32),
        grid=(B, L // tl),
        in_specs=[
            pl.BlockSpec((1, tl, D), lambda b, t: (b, t, 0)),
            pl.BlockSpec((1, D), const),
            pl.BlockSpec((1, D), const),
            pl.BlockSpec((D, D), const),
            pl.BlockSpec((1, D), const),
            pl.BlockSpec((1, tl, D), lambda b, t: (b, t, 0)),
            pl.BlockSpec((1, 6, D), (lambda b, t: (b, 0, 0)) if mod_b else (lambda b, t: (0, 0, 0))),
        ],
        out_specs=pl.BlockSpec((1, tl, D), lambda b, t: (b, t, 0)),
        compiler_params=_cparams(("parallel", "parallel")),
        name="conformer_ln_pw2",
    )(u, lng, lnb, w2, b2, x, mod)


@functools.lru_cache(maxsize=None)
def _dft_tables_host(L):
    idx = np.arange(L, dtype=np.int64)
    ang = ((idx[:, None] * idx[None, :]) % (2 * L)).astype(np.float64) * (np.pi / L)
    return np.cos(ang).astype(BF16), np.sin(ang).astype(BF16)


def _dft_tables(L):
    cos_t, sin_t = _dft_tables_host(L)
    return jnp.asarray(cos_t), jnp.asarray(sin_t)


def _filter_features(L):
    t = jnp.linspace(0.0, 1.0, L, dtype=F32)[:, None]
    bands = (POS_EMB - 1) // 2
    w = 2.0 * math.pi * jnp.arange(L, dtype=F32) / L
    f = jnp.linspace(1e-4, bands - 1, bands, dtype=F32)
    fw = w[:, None] * f[None, :]
    z = jnp.concatenate([t, jnp.cos(fw), -jnp.sin(fw)], axis=-1)
    z = jnp.concatenate([z, jnp.zeros((L, LANES - POS_EMB), F32)], axis=-1)
    return t, z.astype(BF16)


def _hy_filter_kernel(z_ref, t_ref, dl_ref, w1_ref, b1_ref, w2_ref, b2_ref, w3f_ref, w3b_ref, fr_ref,
                      cos_ref, sin_ref, kc_ref, ks_ref, kny_ref, *, L):
    fr = fr_ref[...]
    hdn = jnp.sin(fr * (_dot(z_ref[...], w1_ref[...]) + b1_ref[...]))
    hdn = jnp.sin(fr * (_dot(hdn.astype(BF16), w2_ref[...]) + b2_ref[...])).astype(BF16)
    decay = jnp.exp(-t_ref[...] * jnp.abs(dl_ref[...]))
    row = lax.broadcasted_iota(jnp.int32, (L, 1), 0)
    h_fwd = _dot(hdn, w3f_ref[...]) * decay
    h_bwd = jnp.where(row > 0, _dot(hdn, w3b_ref[...]) * decay, 0.0)
    nrm = (jnp.sum(jnp.abs(h_fwd), axis=0, keepdims=True) + jnp.sum(jnp.abs(h_bwd), axis=0, keepdims=True) + EPS)
    inv = 1.0 / nrm
    ksum = (h_fwd + h_bwd) * inv
    kdif = (h_fwd - h_bwd) * inv
    n = 2 * L
    wcol = jnp.where(row == 0, 1.0 / n, 2.0 / n)
    sgn = jnp.where(jnp.bitwise_and(row, 1) == 0, 1.0, -1.0)

    def split_dot(tab, kk):
        hi = kk.astype(BF16)
        lo = (kk - hi.astype(F32)).astype(BF16)
        return _dot(tab, hi) + _dot(tab, lo)

    kc_ref[...] = split_dot(cos_ref[...], ksum) * wcol
    ks_ref[...] = -split_dot(sin_ref[...], kdif) * wcol
    kny_ref[...] = jnp.sum(sgn * ksum, axis=0, keepdims=True) * (1.0 / n)


def _hy_in_kernel(x_ref, mod_ref, gain_ref, win_ref, bin_ref, wsh_ref, bsh_ref, x0_ref, vx_ref, h_ref, pad_ref,
                  *, L, R, tc):
    c = pl.program_id(1)
    H = CONV_PAD
    D = x_ref.shape[-1]
    n = L // R

    @pl.when(c == 0)
    def _():
        m = mod_ref[0]
        h_ref[0:H, :] = jnp.zeros((H, D), BF16)
        h_ref[H:H + L, :] = _modulate(x_ref[0], gain_ref[...], m[0:1], m[1:2]).astype(BF16)
        h_ref[H + L:2 * H + L, :] = jnp.zeros((H, D), BF16)

    w = wsh_ref[0]

    def project(i):
        return _dot(h_ref[i * R:i * R + R + 2 * H, :], win_ref[0]) + bin_ref[0]

    def short_conv(i, a):
        pad_ref[i] = a
        if i == 0:
            pad_ref[i, 0:H, :] = jnp.zeros((H, 3 * tc), F32)
        if i == n - 1:
            pad_ref[i, H + R:2 * H + R, :] = jnp.zeros((H, 3 * tc), F32)
        u = (pad_ref[i, H - 1:H - 1 + R, :] * w[0:1] + pad_ref[i, H:H + R, :] * w[1:2]
             + pad_ref[i, H + 1:H + 1 + R, :] * w[2:3] + bsh_ref[0])
        x0_ref[0, i * R:(i + 1) * R, :] = u[:, :tc]
        vx_ref[0, i * R:(i + 1) * R, :] = u[:, 2 * tc:] * u[:, tc:2 * tc]

    a = {0: project(0)}
    for i in range(n):
        if i + 1 < n:
            a[i + 1] = project(i + 1)
        short_conv(i, a.pop(i))


def _hy_conv_kernel(vx_ref, x0_ref, cr_ref, sr_ref, cc_ref, sc_ref, kc_ref, ks_ref, kny_ref, skip_ref, o_ref,
                    u_ref, acc_ref, *, L):
    f = pl.program_id(2)
    last = pl.num_programs(2) - 1

    @pl.when(f == 0)
    def _():
        u_ref[...] = vx_ref[0].astype(BF16)

    u = u_ref[...]
    tfq = cr_ref.shape[0]
    parts = 2 if tfq % 256 == 0 else 1
    hp = tfq // parts

    def forward(p):
        rows = slice(p * hp, (p + 1) * hp)
        return _dot(cr_ref[rows, :], u), _dot(sr_ref[rows, :], u)

    def spectral(p, pc, ps):
        rows = slice(p * hp, (p + 1) * hp)
        kc = kc_ref[rows, :]
        ks = ks_ref[rows, :]
        return (pc * kc + ps * ks).astype(BF16), (ps * kc - pc * ks).astype(BF16)

    def inverse(p, zc, zs):
        cols = slice(p * hp, (p + 1) * hp)
        return _dot(cc_ref[:, cols], zc) + _dot(sc_ref[:, cols], zs)

    fw = [forward(p) for p in range(parts)]
    y = None
    for p in range(parts):
        yp = inverse(p, *spectral(p, *fw[p]))
        y = yp if y is None else y + yp

    @pl.when(f == 0)
    def _():
        acc_ref[...] = y

    @pl.when(f > 0)
    def _():
        acc_ref[...] += y

    @pl.when(f == last)
    def _():
        vx = vx_ref[0]
        row = lax.broadcasted_iota(jnp.int32, (L, 1), 0)
        sgn = jnp.where(jnp.bitwise_and(row, 1) == 0, 1.0, -1.0)
        u_ny = jnp.sum(sgn * vx, axis=0, keepdims=True)
        yy = acc_ref[...] + sgn * (u_ny * kny_ref[...])
        o_ref[0] = ((yy + skip_ref[...] * vx) * x0_ref[0]).astype(o_ref.dtype)


def _prep_hyena(w_in, b_in, w_short, b_short, f_w1, f_b1, f_w2, f_b2, f_w3, sin_freq, skip, w_out, b_out):
    D = w_in.shape[0]
    tc = CH_TILE
    nc = D // tc
    winr = w_in.reshape(D, 3, nc, tc).transpose(2, 0, 1, 3).reshape(nc, D, 3 * tc).astype(BF16)
    binr = b_in.reshape(3, nc, tc).transpose(1, 0, 2).reshape(nc, 1, 3 * tc)
    wshr = w_short.reshape(3, 3, nc, tc).transpose(2, 0, 1, 3).reshape(nc, 3, 3 * tc)
    bshr = b_short.reshape(3, nc, tc).transpose(1, 0, 2).reshape(nc, 1, 3 * tc)
    fw1 = jnp.concatenate([f_w1, jnp.zeros((LANES - POS_EMB, FILTER_FO), F32)], axis=0).astype(BF16)
    deltas = jnp.linspace(math.log(DECAY_TARGET) / DECAY_FAST, math.log(DECAY_TARGET) / DECAY_SLOW, D,
                          dtype=F32).reshape(1, D)
    return (winr, binr, wshr, bshr, fw1, f_b1.reshape(1, -1), f_w2.astype(BF16), f_b2.reshape(1, -1),
            f_w3.astype(BF16), sin_freq.reshape(1, -1), deltas, skip.reshape(1, D), w_out.astype(BF16),
            b_out.reshape(1, D))


def _hyena_filter(L, hw):
    (_, _, _, _, fw1, fb1, fw2, fb2, fw3, fr, deltas, _, _, _) = hw
    D = deltas.shape[1]
    tc = CH_TILE
    nc = D // tc
    t, z = _filter_features(L)
    cos_t, sin_t = _dft_tables(L)
    const = lambda c: (0, 0)
    kc, ks, kny = pl.pallas_call(
        functools.partial(_hy_filter_kernel, L=L),
        out_shape=(jax.ShapeDtypeStruct((L, D), F32), jax.ShapeDtypeStruct((L, D), F32),
                   jax.ShapeDtypeStruct((1, D), F32)),
        grid=(nc,),
        in_specs=[
            pl.BlockSpec(z.shape, const),
            pl.BlockSpec(t.shape, const),
            pl.BlockSpec((1, tc), lambda c: (0, c)),
            pl.BlockSpec(fw1.shape, const),
            pl.BlockSpec(fb1.shape, const),
            pl.BlockSpec(fw2.shape, const),
            pl.BlockSpec(fb2.shape, const),
            pl.BlockSpec((FILTER_FO, tc), lambda c: (0, c)),
            pl.BlockSpec((FILTER_FO, tc), lambda c: (0, nc + c)),
            pl.BlockSpec(fr.shape, const),
            pl.BlockSpec((L, L), const),
            pl.BlockSpec((L, L), const),
        ],
        out_specs=(pl.BlockSpec((L, tc), lambda c: (0, c)), pl.BlockSpec((L, tc), lambda c: (0, c)),
                   pl.BlockSpec((1, tc), lambda c: (0, c))),
        compiler_params=_cparams(("arbitrary",), VMEM_BIG),
        name="hyena_filter",
    )(z, t, deltas, fw1, fb1, fw2, fb2, fw3, fw3, fr, cos_t, sin_t)
    return cos_t, sin_t, kc, ks, kny


LCONV_TC = 512
LCONV_TF = 512


def _hyena(x, mod, gain, hw, filt):
    (winr, binr, wshr, bshr, _, _, _, _, _, _, _, skip, w_out, b_out) = hw
    cos_t, sin_t, kc, ks, kny = filt
    B, L, D = x.shape
    tc = CH_TILE
    nc = D // tc
    mod_b = mod.shape[0] > 1
    R = min(CONF_ROWS, L)
    x0, vx = pl.pallas_call(
        functools.partial(_hy_in_kernel, L=L, R=R, tc=tc),
        out_shape=(jax.ShapeDtypeStruct((B, L, D), F32), jax.ShapeDtypeStruct((B, L, D), F32)),
        grid=(B, nc),
        in_specs=[
            pl.BlockSpec((1, L, D), lambda b, c: (b, 0, 0)),
            pl.BlockSpec((1, 6, D), (lambda b, c: (b, 0, 0)) if mod_b else (lambda b, c: (0, 0, 0))),
            pl.BlockSpec((1, D), lambda b, c: (0, 0)),
            pl.BlockSpec((1, D, 3 * tc), lambda b, c: (c, 0, 0)),
            pl.BlockSpec((1, 1, 3 * tc), lambda b, c: (c, 0, 0)),
            pl.BlockSpec((1, 3, 3 * tc), lambda b, c: (c, 0, 0)),
            pl.BlockSpec((1, 1, 3 * tc), lambda b, c: (c, 0, 0)),
        ],
        out_specs=(pl.BlockSpec((1, L, tc), lambda b, c: (b, 0, c)), pl.BlockSpec((1, L, tc), lambda b, c: (b, 0, c))),
        scratch_shapes=[pltpu.VMEM((L + 2 * CONV_PAD, D), BF16),
                        pltpu.VMEM((L // R, R + 2 * CONV_PAD, 3 * tc), F32)],
        compiler_params=_cparams(("parallel", "arbitrary"), VMEM_BIG),
        name="hyena_in_shortconv",
    )(x, mod, gain, winr, binr, wshr, bshr)

    tcl = min(LCONV_TC, D)
    tfq = min(LCONV_TF, L)
    y = pl.pallas_call(
        functools.partial(_hy_conv_kernel, L=L),
        out_shape=jax.ShapeDtypeStruct((B, L, D), BF16),
        grid=(B, D // tcl, L // tfq),
        in_specs=[
            pl.BlockSpec((1, L, tcl), lambda b, c, f: (b, 0, c)),
            pl.BlockSpec((1, L, tcl), lambda b, c, f: (b, 0, c)),
            pl.BlockSpec((tfq, L), lambda b, c, f: (f, 0)),
            pl.BlockSpec((tfq, L), lambda b, c, f: (f, 0)),
            pl.BlockSpec((L, tfq), lambda b, c, f: (0, f)),
            pl.BlockSpec((L, tfq), lambda b, c, f: (0, f)),
            pl.BlockSpec((tfq, tcl), lambda b, c, f: (f, c)),
            pl.BlockSpec((tfq, tcl), lambda b, c, f: (f, c)),
            pl.BlockSpec((1, tcl), lambda b, c, f: (0, c)),
            pl.BlockSpec((1, tcl), lambda b, c, f: (0, c)),
        ],
        out_specs=pl.BlockSpec((1, L, tcl), lambda b, c, f: (b, 0, c)),
        scratch_shapes=[pltpu.VMEM((L, tcl), BF16), pltpu.VMEM((L, tcl), F32)],
        compiler_params=_cparams(("parallel", "parallel", "arbitrary"), VMEM_BIG),
        name="hyena_longconv",
    )(vx, x0, cos_t, sin_t, cos_t, sin_t, kc, ks, kny, skip)
    return _proj_resid(y, w_out, b_out, x, mod, 2, 512)


def kernel(x, c, ctx, c_ctx, ada_w, ada_b, norm_mix, norm_ffn, mla_w_dq, mla_q_norm, mla_w_uq, mla_w_dkv, mla_kv_norm, mla_w_ukv, mla_qk_gain, mla_w_o, cf_w_pw1, cf_b_pw1, cf_w_dw, cf_b_dw, cf_ln_g, cf_ln_b, cf_w_pw2, cf_b_pw2, hy_w_in, hy_b_in, hy_w_short, hy_b_short, hy_f_w1, hy_f_b1, hy_f_w2, hy_f_b2, hy_f_w3, hy_sin_freq, hy_skip, hy_w_out, hy_b_out, ffn_w_up, ffn_w_dw, ffn_b_dw, ffn_w_down):
    B, L, D = x.shape
    Lc = ctx.shape[1]
    depth = ada_w.shape[0]

    rows = ((B + 1 + 7) // 8) * 8
    cvec = jnp.concatenate([c, c_ctx[None, :], jnp.zeros((rows - B - 1, D), F32)], axis=0)
    ada = _ada_all(cvec, ada_w, ada_b)
    zero_bias = jnp.zeros((1, D), F32)
    rope = None
    fw = _prep_ffn(ffn_w_up, ffn_w_dw, ffn_b_dw, ffn_w_down)

    for i in range(depth):
        kind = i % N_MIXERS
        j = i // N_MIXERS
        need_ctx_out = i < depth - 1
        modl = ada[i, :B].reshape(B, 6, D)
        modc = ada[i, B:B + 1].reshape(1, 6, D)
        gmix = norm_mix[i].reshape(1, D)
        gffn = norm_ffn[i].reshape(1, D)
        xc = None
        if kind == 0:
            mw = _prep_mla(mla_w_dq[j], mla_q_norm[j], mla_w_uq[j], mla_w_dkv[j], mla_kv_norm[j], mla_w_ukv[j],
                           mla_qk_gain[j], mla_w_o[j])
            if rope is None:
                rope = _rope_table(Lc, L)
            q, k, vt = _mla_prep(ctx, x, modl, modc, gmix, mw, rope)
            o = _attention(q, k, vt, 0, L, min(ATTN_TQ, L), 0, Lc + L)
            x_new = _proj_resid(o, mw[-1], zero_bias, x, modl, 2, 512)
            if need_ctx_out:
                oc = _attention(q, k, vt, L // Lc, Lc, Lc, L // Lc, Lc)
                xc = _proj_resid(oc, mw[-1], zero_bias, ctx, modc, 2, 512)
            x = x_new
        elif kind == 1:
            cw = _prep_conf(cf_w_pw1[j], cf_b_pw1[j], cf_w_dw[j], cf_b_dw[j], cf_ln_g[j], cf_ln_b[j], cf_w_pw2[j],
                            cf_b_pw2[j])
            x = _conformer(x, modl, gmix, cw)
            if need_ctx_out:
                xc = _conformer(ctx, modc, gmix, cw)
        else:
            hw = _prep_hyena(hy_w_in[j], hy_b_in[j], hy_w_short[j], hy_b_short[j], hy_f_w1[j], hy_f_b1[j],
                             hy_f_w2[j], hy_f_b2[j], hy_f_w3[j], hy_sin_freq[j], hy_skip[j], hy_w_out[j],
                             hy_b_out[j])
            x = _hyena(x, modl, gmix, hw, _hyena_filter(L, hw))
            if need_ctx_out:
                xc = _hyena(ctx, modc, gmix, hw, _hyena_filter(Lc, hw))
        x = _ffn(x, modl, gffn, fw, i)
        if need_ctx_out:
            ctx = _ffn(xc, modc, gffn, fw, i)
    return x
```

```python
import functools
import math

import numpy as np
import jax
import jax.numpy as jnp
from jax import lax
from jax.experimental import pallas as pl
from jax.experimental.pallas import tpu as pltpu

F32 = jnp.float32
BF16 = jnp.bfloat16

D_MODEL = 1024
DEPTH = 4
GRID_W = 64
N_MIXERS = 3
EPS = 1e-6
N_HEADS = 16
QK_NOPE = 64
QK_ROPE = 32
V_DIM = 64
Q_LORA = 256
KV_LORA = 128
ROPE_THETA = 10000.0
CONV_W = 31
POS_EMB = 33
FILTER_FO = 64
DECAY_FAST = 0.3
DECAY_SLOW = 1.5
DECAY_TARGET = 1e-2
D_FF = 2816

LANES = 128
HEAD_SLOT = 128
MIB = 1024 * 1024
VMEM_BIG = 58 * MIB
VMEM_MID = 48 * MIB


def _cparams(sem, vmem=VMEM_MID):
    return pltpu.CompilerParams(dimension_semantics=sem, vmem_limit_bytes=vmem)


def _sigmoid(x):
    return 1.0 / (1.0 + jnp.exp(-x))


def _silu(x):
    return x * _sigmoid(x)


def _rms_scale(x, n):
    return lax.rsqrt(jnp.sum(x * x, axis=-1, keepdims=True) * (1.0 / n) + EPS)


def _modulate(x, gain, shift, scale):
    y = x * _rms_scale(x, x.shape[-1]) * gain
    return y * (1.0 + scale) + shift


def _dot(a, b):
    return jnp.dot(a, b, preferred_element_type=F32)


def _ada_kernel(c_ref, w_ref, b_ref, o_ref):
    s = _silu(c_ref[...]).astype(BF16)
    o_ref[0] = _dot(s, w_ref[0].astype(BF16)) + b_ref[0]


def _ada_all(cvec, ada_w, ada_b):
    rows = cvec.shape[0]
    depth, d, n = ada_w.shape
    tn = 1536
    return pl.pallas_call(
        _ada_kernel,
        out_shape=jax.ShapeDtypeStruct((depth, rows, n), F32),
        grid=(depth, n // tn),
        in_specs=[
            pl.BlockSpec((rows, d), lambda i, j: (0, 0)),
            pl.BlockSpec((1, d, tn), lambda i, j: (i, 0, j)),
            pl.BlockSpec((1, 1, tn), lambda i, j: (i, 0, j)),
        ],
        out_specs=pl.BlockSpec((1, rows, tn), lambda i, j: (i, 0, j)),
        compiler_params=_cparams(("parallel", "parallel")),
        name="ada_mod",
    )(cvec, ada_w, ada_b.reshape(depth, 1, n))


FFN_TILE = 256


FFN_HALO = 16
FFN_ROWS = 512


def _ffn_kernel(*refs, R, tf, whole_seq, mixer_proj):
    if mixer_proj:
        (xp_ref, x_ref, xn_ref, ap_ref, a_ref, an_ref, wp_ref, bp_ref, mod_ref, gain_ref, wup_ref, wdw_ref, bdw_ref,
         wdn_ref, o_ref, h_ref, pad_ref, act_ref, acat_ref, xin_ref) = refs
    else:
        (xp_ref, x_ref, xn_ref, mod_ref, gain_ref, wup_ref, wdw_ref, bdw_ref, wdn_ref, o_ref,
         h_ref, pad_ref, act_ref) = refs
        xin_ref = None
    c = pl.program_id(1)
    nc = pl.num_programs(1)
    H = FFN_HALO
    D = x_ref.shape[-1]
    nj = D_FF // tf
    m = mod_ref[0]
    gain = gain_ref[...]

    def modulated(v):
        return _modulate(v, gain, m[3:4], m[4:5])

    x_main, x_prev, x_next = x_ref[0], None, None
    if mixer_proj and whole_seq:
        x_main = x_main + m[2:3] * (_dot(a_ref[0], wp_ref[...]) + bp_ref[...])
    elif mixer_proj:
        acat_ref[0:H, :] = ap_ref[0]
        acat_ref[H:H + R, :] = a_ref[0]
        acat_ref[H + R:2 * H + R, :] = an_ref[0]
        proj = m[2:3] * (_dot(acat_ref[...], wp_ref[...]) + bp_ref[...])
        x_main = x_main + proj[H:H + R]
        x_prev = xp_ref[0] + proj[0:H]
        x_next = xn_ref[0] + proj[H + R:2 * H + R]
    elif not whole_seq:
        x_prev, x_next = xp_ref[0], xn_ref[0]
    if mixer_proj:
        xin_ref[...] = x_main

    h_ref[H:H + R, :] = modulated(x_main).astype(BF16)
    if whole_seq:
        h_ref[0:H, :] = jnp.zeros((H, D), BF16)
        h_ref[H + R:2 * H + R, :] = jnp.zeros((H, D), BF16)
    else:
        h_ref[0:H, :] = jnp.where(c > 0, modulated(x_prev), 0.0).astype(BF16)
        h_ref[H + R:2 * H + R, :] = jnp.where(c < nc - 1, modulated(x_next), 0.0).astype(BF16)

    def up(t):
        cols = slice(t * tf, (t + 1) * tf)
        pad_ref[t] = _dot(h_ref[...], wup_ref[0, :, cols])
        return _dot(h_ref[H:H + R, :], wup_ref[0, :, D_FF + t * tf:D_FF + (t + 1) * tf])

    def gate(t, lin):
        cols = slice(t * tf, (t + 1) * tf)
        w = wdw_ref[0, :, cols]
        g = (pad_ref[t, H - 1:H - 1 + R, :] * w[0:1] + pad_ref[t, H:H + R, :] * w[1:2]
             + pad_ref[t, H + 1:H + 1 + R, :] * w[2:3] + bdw_ref[0, :, cols])
        act_ref[:, cols] = (_silu(g) * lin).astype(BF16)

    lin = {0: up(0)}
    for t in range(nj):
        if t + 1 < nj:
            lin[t + 1] = up(t + 1)
        gate(t, lin.pop(t))
    y = _dot(act_ref[...], wdn_ref[0])
    o_ref[0] = (xin_ref[...] if mixer_proj else x_ref[0]) + m[5:6] * y


def _prep_ffn(w_up, w_dw, b_dw, w_down):
    depth = w_up.shape[0]
    return w_up.astype(BF16), w_dw, b_dw.reshape(depth, 1, D_FF), w_down.astype(BF16)


def _ffn(x, mod, gain, ffn_w, layer, mixer=None):
    wup, wdw, bdw, wdn = ffn_w
    B, L, D = x.shape
    tf = FFN_TILE
    nj = D_FF // tf
    mod_b = mod.shape[0] > 1
    R = min(FFN_ROWS, L)
    H = FFN_HALO
    hb = R // H
    last_hb = L // H - 1
    resident = pl.Buffered(1)

    def row_specs(width):
        return [pl.BlockSpec((1, H, width), lambda b, c: (b, jnp.maximum(c * hb - 1, 0), 0)),
                pl.BlockSpec((1, R, width), lambda b, c: (b, c, 0)),
                pl.BlockSpec((1, H, width), lambda b, c: (b, jnp.minimum((c + 1) * hb, last_hb), 0))]

    in_specs = row_specs(D)
    args = [x, x, x]
    scratch = [pltpu.VMEM((R + 2 * H, D), BF16), pltpu.VMEM((nj, R + 2 * H, tf), F32), pltpu.VMEM((R, D_FF), BF16)]
    if mixer is not None:
        a, wp, bp = mixer
        K = a.shape[-1]
        in_specs += row_specs(K) + [pl.BlockSpec((K, D), lambda b, c: (0, 0), pipeline_mode=resident),
                                    pl.BlockSpec((1, D), lambda b, c: (0, 0))]
        args += [a, a, a, wp, bp]
        scratch += [pltpu.VMEM((R + 2 * H, K), BF16), pltpu.VMEM((R, D), F32)]
    in_specs += [
        pl.BlockSpec((1, 6, D), (lambda b, c: (b, 0, 0)) if mod_b else (lambda b, c: (0, 0, 0))),
        pl.BlockSpec((1, D), lambda b, c: (0, 0)),
        pl.BlockSpec((1, D, 2 * D_FF), lambda b, c: (layer, 0, 0), pipeline_mode=resident),
        pl.BlockSpec((1, 3, D_FF), lambda b, c: (layer, 0, 0)),
        pl.BlockSpec((1, 1, D_FF), lambda b, c: (layer, 0, 0)),
        pl.BlockSpec((1, D_FF, D), lambda b, c: (layer, 0, 0), pipeline_mode=resident),
    ]
    args += [mod, gain, wup, wdw, bdw, wdn]
    return pl.pallas_call(
        functools.partial(_ffn_kernel, R=R, tf=tf, whole_seq=(R == L), mixer_proj=mixer is not None),
        out_shape=jax.ShapeDtypeStruct((B, L, D), F32),
        grid=(B, L // R),
        in_specs=in_specs,
        out_specs=pl.BlockSpec((1, R, D), lambda b, c: (b, c, 0)),
        scratch_shapes=scratch,
        compiler_params=_cparams(("parallel", "parallel"), VMEM_BIG),
        name="conv_ffn",
    )(*args)


MLA_TL = 256
MLA_SUB = 128


def _rope_partner(j):
    return j + 8 if (j % 16) < 8 else j - 8


def _rope_table(n_ctx, L):
    rows = L // GRID_W
    row = jnp.repeat(jnp.arange(rows), GRID_W).astype(F32)
    col = jnp.tile(jnp.arange(GRID_W), rows).astype(F32)
    half = QK_ROPE // 2
    inv = ROPE_THETA ** (-(jnp.arange(0, half, 2, dtype=F32) / half))
    ang = jnp.concatenate([row[:, None] * inv, col[:, None] * inv], axis=-1)
    ang = jnp.concatenate([ang, jnp.zeros((n_ctx, QK_ROPE // 2), F32)], axis=0)
    idx = jnp.array([(j // 16) * 8 + (j % 8) for j in range(QK_ROPE)])
    sgn = jnp.array([-1.0 if (j % 16) < 8 else 1.0 for j in range(QK_ROPE)], F32)
    a = ang[:, idx]
    ones = jnp.ones((ang.shape[0], QK_NOPE), F32)
    return jnp.concatenate([ones, jnp.cos(a), jnp.sin(a) * sgn], axis=-1)


def _prep_mla(w_dq, q_norm, w_uq, w_dkv, kv_norm, w_ukv, qk_gain, w_o):
    H = N_HEADS
    perm = jnp.array([_rope_partner(j) for j in range(QK_ROPE)])
    D = w_dq.shape[0]
    w_kpe = w_dkv[:, KV_LORA:]
    wd = jnp.concatenate([w_dq, w_dkv[:, :KV_LORA], jnp.zeros((D, QK_NOPE), F32), w_kpe, w_kpe[:, perm]],
                         axis=1).astype(BF16)
    wq = w_uq.reshape(Q_LORA, H, QK_NOPE + QK_ROPE)
    wq_pe = wq[:, :, QK_NOPE:]
    wuq = jnp.concatenate([wq, wq_pe[:, :, perm]], axis=-1).reshape(Q_LORA, H * HEAD_SLOT).astype(BF16)
    wkv = w_ukv.reshape(KV_LORA, H, QK_NOPE + V_DIM)
    wk = jnp.concatenate([wkv[:, :, :QK_NOPE], jnp.zeros((KV_LORA, H, HEAD_SLOT - QK_NOPE), F32)], axis=-1)
    wuk = wk.reshape(KV_LORA, H * HEAD_SLOT).astype(BF16)
    wuv_t = wkv[:, :, QK_NOPE:].reshape(KV_LORA, H * V_DIM).T.astype(BF16)
    scale = (QK_NOPE + QK_ROPE) ** -0.5 * math.log2(math.e)
    cn, cp = math.sqrt(QK_NOPE), math.sqrt(QK_ROPE)
    gq = qk_gain[0]
    gq_slot = jnp.concatenate([gq[:QK_NOPE] * cn, gq[QK_NOPE:] * cp, gq[QK_NOPE:][perm] * cp]) * scale
    gk = qk_gain[1]
    gk_pe_slot = jnp.concatenate([jnp.zeros((QK_NOPE,), F32), gk[QK_NOPE:], gk[QK_NOPE:][perm]]) * cp
    gk_n_slot = jnp.concatenate([gk[:QK_NOPE] * cn, jnp.zeros((HEAD_SLOT - QK_NOPE,), F32)])
    gains = jnp.stack([gq_slot, gk_pe_slot, gk_n_slot], axis=0)
    return (wd, q_norm.reshape(1, Q_LORA), kv_norm.reshape(1, KV_LORA), wuq, wuk, wuv_t, gains, w_o.astype(BF16))


def _mla_prep_kernel(ctx_ref, x_ref, modl_ref, modc_ref, gain_ref, wd_ref, qn_ref, kvn_ref, wuq_ref, wuk_ref,
                     wuvt_ref, gkn_ref, tab_ref, q_ref, k_ref, vt_ref):
    is_ctx = pl.program_id(1) == pl.num_programs(1) - 1
    m = jnp.where(is_ctx, modc_ref[0], modl_ref[0])
    lane = lax.broadcasted_iota(jnp.int32, (1, HEAD_SLOT), 1)
    nope_mask = lane < QK_NOPE
    pe_mask = jnp.logical_and(lane >= QK_NOPE, lane < QK_NOPE + QK_ROPE)
    g_kn = gkn_ref[...]
    fold = HEAD_SLOT - QK_ROPE
    tl = x_ref.shape[1]
    parts = tl // MLA_SUB

    def sumsq(v):
        return jnp.sum(v * v, axis=-1, keepdims=True)

    def project(p):
        rs = slice(p * MLA_SUB, (p + 1) * MLA_SUB)
        xin = jnp.where(is_ctx, ctx_ref[0, rs, :], x_ref[0, rs, :])
        h = _modulate(xin, gain_ref[...], m[0:1], m[1:2]).astype(BF16)
        d = _dot(h, wd_ref[...])
        cq = d[:, :Q_LORA]
        cq = (cq * _rms_scale(cq, Q_LORA) * qn_ref[...]).astype(BF16)
        ckv = d[:, Q_LORA:Q_LORA + KV_LORA]
        ckv = (ckv * _rms_scale(ckv, KV_LORA) * kvn_ref[...]).astype(BF16)
        kp = d[:, Q_LORA + KV_LORA:]
        xk = kp * lax.rsqrt(0.5 * sumsq(kp) + QK_ROPE * EPS)
        kpe_slot = (xk * tab_ref[rs, 2 * HEAD_SLOT:3 * HEAD_SLOT]
                    + pltpu.roll(xk * tab_ref[rs, 3 * HEAD_SLOT:4 * HEAD_SLOT], fold, 1))
        qraw = _dot(cq, wuq_ref[...])
        kvu = _dot(ckv, wuk_ref[...])
        vt_ref[0, :, rs] = lax.dot_general(wuvt_ref[...], ckv, (((1,), (1,)), ((), ())),
                                           preferred_element_type=F32).astype(BF16)
        return qraw, kvu, kpe_slot

    def heads(p, qraw, kvu, kpe_slot):
        rs = slice(p * MLA_SUB, (p + 1) * MLA_SUB)
        tq_a = tab_ref[rs, 0 * HEAD_SLOT:1 * HEAD_SLOT]
        tq_b = tab_ref[rs, 1 * HEAD_SLOT:2 * HEAD_SLOT]

        def norms(hd):
            sl = slice(hd * HEAD_SLOT, (hd + 1) * HEAD_SLOT)
            sq = qraw[:, sl] * qraw[:, sl]
            rn = lax.rsqrt(jnp.sum(jnp.where(nope_mask, sq, 0.0), axis=-1, keepdims=True) + QK_NOPE * EPS)
            rp = lax.rsqrt(jnp.sum(jnp.where(pe_mask, sq, 0.0), axis=-1, keepdims=True) + QK_ROPE * EPS)
            rk = lax.rsqrt(sumsq(kvu[:, sl]) + QK_NOPE * EPS)
            return rn, rp, rk

        def emit(hd, rn, rp, rk):
            sl = slice(hd * HEAD_SLOT, (hd + 1) * HEAD_SLOT)
            xq = qraw[:, sl] * jnp.where(nope_mask, rn, rp)
            q_ref[0, rs, sl] = (xq * tq_a + pltpu.roll(xq * tq_b, fold, 1)).astype(BF16)
            k_ref[0, rs, sl] = (kvu[:, sl] * rk * g_kn + kpe_slot).astype(BF16)

        ahead = 2
        stats = {hd: norms(hd) for hd in range(min(ahead, N_HEADS))}
        for hd in range(N_HEADS):
            if hd + ahead < N_HEADS:
                stats[hd + ahead] = norms(hd + ahead)
            emit(hd, *stats.pop(hd))

    proj = {0: project(0)}
    for p in range(parts):
        if p + 1 < parts:
            proj[p + 1] = project(p + 1)
        heads(p, *proj.pop(p))


def _mla_prep(ctx, x, modl, modc, gain, mw, rope):
    wd, qn, kvn, wuq, wuk, wuv_t, gains, _ = mw
    B, L, D = x.shape
    Lc = ctx.shape[1]
    tl = MLA_TL
    assert Lc == tl and L % tl == 0
    nt = (Lc + L) // tl
    Lt = Lc + L
    H = N_HEADS
    const = lambda b, t: (0, 0)
    partner_lanes = jnp.arange(HEAD_SLOT) >= QK_NOPE + QK_ROPE
    tq = rope * gains[0:1]
    tk = rope * gains[1:2]
    tabs = jnp.concatenate([jnp.where(partner_lanes, 0.0, tq), jnp.where(partner_lanes, tq, 0.0),
                            jnp.where(partner_lanes, 0.0, tk), jnp.where(partner_lanes, tk, 0.0)], axis=1)
    gkn = gains[2:3]
    return pl.pallas_call(
        _mla_prep_kernel,
        out_shape=(jax.ShapeDtypeStruct((B, Lt, H * HEAD_SLOT), BF16),
                   jax.ShapeDtypeStruct((B, Lt, H * HEAD_SLOT), BF16),
                   jax.ShapeDtypeStruct((B, H * V_DIM, Lt), BF16)),
        grid=(B, nt),
        in_specs=[
            pl.BlockSpec((1, tl, D), lambda b, t: (b, 0, 0)),
            pl.BlockSpec((1, tl, D), lambda b, t: (b, jnp.minimum(t, nt - 2), 0)),
            pl.BlockSpec((1, 6, D), lambda b, t: (b, 0, 0)),
            pl.BlockSpec((1, 6, D), lambda b, t: (0, 0, 0)),
            pl.BlockSpec((1, D), const),
            pl.BlockSpec(wd.shape, const),
            pl.BlockSpec(qn.shape, const),
            pl.BlockSpec(kvn.shape, const),
            pl.BlockSpec(wuq.shape, const),
            pl.BlockSpec(wuk.shape, const),
            pl.BlockSpec(wuv_t.shape, const),
            pl.BlockSpec(gkn.shape, const),
            pl.BlockSpec((tl, 4 * HEAD_SLOT), lambda b, t: (t, 0)),
        ],
        out_specs=(pl.BlockSpec((1, tl, H * HEAD_SLOT), lambda b, t: (b, t, 0)),
                   pl.BlockSpec((1, tl, H * HEAD_SLOT), lambda b, t: (b, t, 0)),
                   pl.BlockSpec((1, H * V_DIM, tl), lambda b, t: (b, 0, t))),
        compiler_params=_cparams(("parallel", "parallel")),
        name="mla_prep",
    )(ctx, x, modl, modc, gain, wd, qn, kvn, wuq, wuk, wuv_t, gkn, tabs)


HEADS_PER_STEP = 2
ATTN_TQ = 2048


ATTN_QCOLS = 256
ATTN_KEY_BLOCK = 128


def _attn_kernel(q_ref, k_ref, vt_ref, o_ref, st_ref, p_ref):
    tq = q_ref.shape[1]
    chains = [(qc, hh) for qc in range(tq // ATTN_QCOLS) for hh in range(HEADS_PER_STEP)]
    n = len(chains)

    nk = k_ref.shape[1]
    s_slots = st_ref.shape[0]
    p_slots = p_ref.shape[0]
    kb = ATTN_KEY_BLOCK
    grp = (kb // 8, 8, ATTN_QCOLS)

    def scores(i):
        qc, hh = chains[i]
        sl = slice(hh * HEAD_SLOT, (hh + 1) * HEAD_SLOT)
        st_ref[i % s_slots] = lax.dot_general(
            k_ref[0, :, sl], q_ref[0, qc * ATTN_QCOLS:(qc + 1) * ATTN_QCOLS, sl],
            (((1,), (1,)), ((), ())), preferred_element_type=F32)

    def softmax(i):
        s_slot, p_slot = i % s_slots, i % p_slots
        mx = jnp.full(grp[1:], -jnp.inf, F32)
        for r0 in range(0, nk, kb):
            mx = jnp.maximum(mx, jnp.max(st_ref[s_slot, r0:r0 + kb, :].reshape(grp), axis=0))
        m = jnp.max(mx, axis=0, keepdims=True)
        acc = jnp.zeros(grp[1:], F32)
        for r0 in range(0, nk, kb):
            p = jnp.exp2(st_ref[s_slot, r0:r0 + kb, :] - m)
            acc = acc + jnp.sum(p.reshape(grp), axis=0)
            p_ref[p_slot, r0:r0 + kb, :] = p.astype(BF16)
        return jnp.sum(acc, axis=0, keepdims=True)

    def weighted_values(i, l):
        hh = chains[i][1]
        return _dot(vt_ref[0, hh * V_DIM:(hh + 1) * V_DIM, :], p_ref[i % p_slots]) / l

    scores(0)
    if n > 1:
        scores(1)
    sm = {0: softmax(0)}
    done = {}
    for i in range(n):
        if i + 2 < n:
            scores(i + 2)
        done[chains[i]] = weighted_values(i, sm.pop(i))
        if i + 1 < n:
            sm[i + 1] = softmax(i + 1)
        qc, hh = chains[i]
        if hh == HEADS_PER_STEP - 1:
            ot = jnp.concatenate([done.pop((qc, h2)) for h2 in range(HEADS_PER_STEP)], axis=0)
            o_ref[0, qc * ATTN_QCOLS:(qc + 1) * ATTN_QCOLS, :] = ot.T.astype(o_ref.dtype)


def _attention(q, k, vt, q_tile0, n_q, tq, k_tile0, n_k):
    B = q.shape[0]
    H = N_HEADS
    nhp = H // HEADS_PER_STEP
    wq = HEADS_PER_STEP * HEAD_SLOT
    wv = HEADS_PER_STEP * V_DIM
    return pl.pallas_call(
        _attn_kernel,
        out_shape=jax.ShapeDtypeStruct((B, n_q, H * V_DIM), BF16),
        grid=(B, nhp, n_q // tq),
        in_specs=[
            pl.BlockSpec((1, tq, wq), lambda b, h, t: (b, q_tile0 + t, h)),
            pl.BlockSpec((1, n_k, wq), lambda b, h, t: (b, k_tile0, h)),
            pl.BlockSpec((1, wv, n_k), lambda b, h, t: (b, h, k_tile0)),
        ],
        out_specs=pl.BlockSpec((1, tq, wv), lambda b, h, t: (b, t, h)),
        scratch_shapes=[pltpu.VMEM((3, n_k, ATTN_QCOLS), F32), pltpu.VMEM((2, n_k, ATTN_QCOLS), BF16)],
        compiler_params=_cparams(("parallel", "parallel", "arbitrary")),
        name="mla_attn",
    )(q, k, vt)


CH_TILE = 256
CONV_PAD = 16
CONV_ROWS = 128
CONF_ROWS = 512


def _conf_a_kernel(x_ref, mod_ref, gain_ref, w1_ref, b1_ref, wdw_ref, bdw_ref, o_ref, h_ref, pad_ref,
                   *, L, R, tc):
    c = pl.program_id(1)
    H = CONV_PAD
    D = x_ref.shape[-1]
    n = L // R

    @pl.when(c == 0)
    def _():
        m = mod_ref[0]
        h_ref[0:H, :] = jnp.zeros((H, D), BF16)
        h_ref[H:H + L, :] = _modulate(x_ref[0], gain_ref[...], m[0:1], m[1:2]).astype(BF16)
        h_ref[H + L:2 * H + L, :] = jnp.zeros((H, D), BF16)

    w = wdw_ref[...]
    bias = bdw_ref[...]
    half = (CONV_W - 1) // 2
    rows = min(CONV_ROWS, R)
    win = rows + 2 * H

    def project(i):
        return _dot(h_ref[i * R:i * R + R + 2 * H, :], w1_ref[0]) + b1_ref[0]

    def glu_conv(i, a):
        pad_ref[i] = a[:, :tc] * _sigmoid(a[:, tc:])
        if i == 0:
            pad_ref[i, 0:H, :] = jnp.zeros((H, tc), F32)
        if i == n - 1:
            pad_ref[i, H + R:2 * H + R, :] = jnp.zeros((H, tc), F32)
        for q0 in range(0, R, rows):
            for l0 in range(0, tc, LANES):
                lanes = slice(l0, l0 + LANES)
                window = pad_ref[i, q0:q0 + win, lanes]
                acc = jnp.broadcast_to(bias[:, lanes], (rows, LANES))
                for r in range(8):
                    rot = window if r == 0 else pltpu.roll(window, win - r, 0)
                    for k in range(CONV_W):
                        off = k + 1 + (H - 1 - half)
                        if off % 8 == r:
                            acc = acc + rot[off - r:off - r + rows, :] * w[k:k + 1, lanes]
                o_ref[0, i * R + q0:i * R + q0 + rows, lanes] = acc

    a = {0: project(0)}
    for i in range(n):
        if i + 1 < n:
            a[i + 1] = project(i + 1)
        glu_conv(i, a.pop(i))


def _conf_b_kernel(u_ref, lng_ref, lnb_ref, w2_ref, b2_ref, x_ref, mod_ref, o_ref):
    u = u_ref[0]
    mu = jnp.mean(u, axis=-1, keepdims=True)
    uc = u - mu
    var = jnp.mean(uc * uc, axis=-1, keepdims=True)
    y = uc * lax.rsqrt(var + EPS) * lng_ref[...] + lnb_ref[...]
    y = _dot(_silu(y).astype(BF16), w2_ref[...]) + b2_ref[...]
    o_ref[0] = x_ref[0] + mod_ref[0, 2:3, :] * y


def _prep_conf(w1, b1, wdw, bdw, lng, lnb, w2, b2):
    D = w1.shape[0]
    tc = CH_TILE
    nc = D // tc
    w1r = w1.reshape(D, 2, nc, tc).transpose(2, 0, 1, 3).reshape(nc, D, 2 * tc).astype(BF16)
    b1r = b1.reshape(2, nc, tc).transpose(1, 0, 2).reshape(nc, 1, 2 * tc)
    return (w1r, b1r, wdw, bdw.reshape(1, D), lng.reshape(1, D), lnb.reshape(1, D), w2.astype(BF16),
            b2.reshape(1, D))


def _conformer(x, mod, gain, cw):
    w1r, b1r, wdw, bdw, lng, lnb, w2, b2 = cw
    B, L, D = x.shape
    tc = CH_TILE
    nc = D // tc
    mod_b = mod.shape[0] > 1
    mod_map2 = (lambda b, c: (b, 0, 0)) if mod_b else (lambda b, c: (0, 0, 0))
    R = min(CONF_ROWS, L)
    u = pl.pallas_call(
        functools.partial(_conf_a_kernel, L=L, R=R, tc=tc),
        out_shape=jax.ShapeDtypeStruct((B, L, D), F32),
        grid=(B, nc),
        in_specs=[
            pl.BlockSpec((1, L, D), lambda b, c: (b, 0, 0)),
            pl.BlockSpec((1, 6, D), mod_map2),
            pl.BlockSpec((1, D), lambda b, c: (0, 0)),
            pl.BlockSpec((1, D, 2 * tc), lambda b, c: (c, 0, 0)),
            pl.BlockSpec((1, 1, 2 * tc), lambda b, c: (c, 0, 0)),
            pl.BlockSpec((CONV_W, tc), lambda b, c: (0, c)),
            pl.BlockSpec((1, tc), lambda b, c: (0, c)),
        ],
        out_specs=pl.BlockSpec((1, L, tc), lambda b, c: (b, 0, c)),
        scratch_shapes=[pltpu.VMEM((L + 2 * CONV_PAD, D), BF16),
                        pltpu.VMEM((L // R, R + 2 * CONV_PAD, tc), F32)],
        compiler_params=_cparams(("parallel", "arbitrary")),
        name="conformer_glu_dwconv",
    )(x, mod, gain, w1r, b1r, wdw, bdw)
    tl = min(512, L)
    const = lambda b, t: (0, 0)
    return pl.pallas_call(
        _conf_b_kernel,
        out_shape=jax.ShapeDtypeStruct((B, L, D), F32),
        grid=(B, L // tl),
        in_specs=[
            pl.BlockSpec((1, tl, D), lambda b, t: (b, t, 0)),
            pl.BlockSpec((1, D), const),
            pl.BlockSpec((1, D), const),
            pl.BlockSpec((D, D), const),
            pl.BlockSpec((1, D), const),
            pl.BlockSpec((1, tl, D), lambda b, t: (b, t, 0)),
            pl.BlockSpec((1, 6, D), (lambda b, t: (b, 0, 0)) if mod_b else (lambda b, t: (0, 0, 0))),
        ],
        out_specs=pl.BlockSpec((1, tl, D), lambda b, t: (b, t, 0)),
        compiler_params=_cparams(("parallel", "parallel")),
        name="conformer_ln_pw2",
    )(u, lng, lnb, w2, b2, x, mod)


@functools.lru_cache(maxsize=None)
def _dft_tables_host(L):
    idx = np.arange(L, dtype=np.int64)
    ang = ((idx[:, None] * idx[None, :]) % (2 * L)).astype(np.float64) * (np.pi / L)
    return np.cos(ang).astype(BF16), np.sin(ang).astype(BF16)


def _dft_tables(L):
    cos_t, sin_t = _dft_tables_host(L)
    return jnp.asarray(cos_t), jnp.asarray(sin_t)


def _filter_features(L):
    t = jnp.linspace(0.0, 1.0, L, dtype=F32)[:, None]
    bands = (POS_EMB - 1) // 2
    w = 2.0 * math.pi * jnp.arange(L, dtype=F32) / L
    f = jnp.linspace(1e-4, bands - 1, bands, dtype=F32)
    fw = w[:, None] * f[None, :]
    z = jnp.concatenate([t, jnp.cos(fw), -jnp.sin(fw)], axis=-1)
    z = jnp.concatenate([z, jnp.zeros((L, LANES - POS_EMB), F32)], axis=-1)
    return t, z.astype(BF16)


def _hy_filter_kernel(z_ref, t_ref, dl_ref, w1_ref, b1_ref, w2_ref, b2_ref, w3f_ref, w3b_ref, fr_ref,
                      cos_ref, sin_ref, kc_ref, ks_ref, kny_ref, *, L):
    fr = fr_ref[...]
    hdn = jnp.sin(fr * (_dot(z_ref[...], w1_ref[...]) + b1_ref[...]))
    hdn = jnp.sin(fr * (_dot(hdn.astype(BF16), w2_ref[...]) + b2_ref[...])).astype(BF16)
    decay = jnp.exp(-t_ref[...] * jnp.abs(dl_ref[...]))
    row = lax.broadcasted_iota(jnp.int32, (L, 1), 0)
    h_fwd = _dot(hdn, w3f_ref[...]) * decay
    h_bwd = jnp.where(row > 0, _dot(hdn, w3b_ref[...]) * decay, 0.0)
    nrm = (jnp.sum(jnp.abs(h_fwd), axis=0, keepdims=True) + jnp.sum(jnp.abs(h_bwd), axis=0, keepdims=True) + EPS)
    inv = 1.0 / nrm
    ksum = (h_fwd + h_bwd) * inv
    kdif = (h_fwd - h_bwd) * inv
    n = 2 * L
    wcol = jnp.where(row == 0, 1.0 / n, 2.0 / n)
    sgn = jnp.where(jnp.bitwise_and(row, 1) == 0, 1.0, -1.0)

    def split_dot(tab, kk):
        hi = kk.astype(BF16)
        lo = (kk - hi.astype(F32)).astype(BF16)
        return _dot(tab, hi) + _dot(tab, lo)

    kc_ref[...] = split_dot(cos_ref[...], ksum) * wcol
    ks_ref[...] = -split_dot(sin_ref[...], kdif) * wcol
    kny_ref[...] = jnp.sum(sgn * ksum, axis=0, keepdims=True) * (1.0 / n)


def _hy_in_kernel(x_ref, mod_ref, gain_ref, win_ref, bin_ref, wsh_ref, bsh_ref, x0_ref, vx_ref, h_ref, pad_ref,
                  *, L, R, tc):
    c = pl.program_id(1)
    H = CONV_PAD
    D = x_ref.shape[-1]
    n = L // R

    @pl.when(c == 0)
    def _():
        m = mod_ref[0]
        h_ref[0:H, :] = jnp.zeros((H, D), BF16)
        h_ref[H:H + L, :] = _modulate(x_ref[0], gain_ref[...], m[0:1], m[1:2]).astype(BF16)
        h_ref[H + L:2 * H + L, :] = jnp.zeros((H, D), BF16)

    w = wsh_ref[0]

    def project(i):
        return _dot(h_ref[i * R:i * R + R + 2 * H, :], win_ref[0]) + bin_ref[0]

    def short_conv(i, a):
        pad_ref[i] = a
        if i == 0:
            pad_ref[i, 0:H, :] = jnp.zeros((H, 3 * tc), F32)
        if i == n - 1:
            pad_ref[i, H + R:2 * H + R, :] = jnp.zeros((H, 3 * tc), F32)
        u = (pad_ref[i, H - 1:H - 1 + R, :] * w[0:1] + pad_ref[i, H:H + R, :] * w[1:2]
             + pad_ref[i, H + 1:H + 1 + R, :] * w[2:3] + bsh_ref[0])
        x0_ref[0, i * R:(i + 1) * R, :] = u[:, :tc]
        vx_ref[0, i * R:(i + 1) * R, :] = u[:, 2 * tc:] * u[:, tc:2 * tc]

    a = {0: project(0)}
    for i in range(n):
        if i + 1 < n:
            a[i + 1] = project(i + 1)
        short_conv(i, a.pop(i))


def _hy_conv_kernel(vx_ref, x0_ref, cr_ref, sr_ref, cc_ref, sc_ref, kc_ref, ks_ref, kny_ref, skip_ref, o_ref,
                    u_ref, acc_ref, *, L):
    f = pl.program_id(2)
    last = pl.num_programs(2) - 1

    @pl.when(f == 0)
    def _():
        u_ref[...] = vx_ref[0].astype(BF16)

    u = u_ref[...]
    tfq = cr_ref.shape[0]
    parts = 2 if tfq % 256 == 0 else 1
    hp = tfq // parts

    def forward(p):
        rows = slice(p * hp, (p + 1) * hp)
        return _dot(cr_ref[rows, :], u), _dot(sr_ref[rows, :], u)

    def spectral(p, pc, ps):
        rows = slice(p * hp, (p + 1) * hp)
        kc = kc_ref[rows, :]
        ks = ks_ref[rows, :]
        return (pc * kc + ps * ks).astype(BF16), (ps * kc - pc * ks).astype(BF16)

    def inverse(p, zc, zs):
        cols = slice(p * hp, (p + 1) * hp)
        return _dot(cc_ref[:, cols], zc) + _dot(sc_ref[:, cols], zs)

    fw = [forward(p) for p in range(parts)]
    y = None
    for p in range(parts):
        yp = inverse(p, *spectral(p, *fw[p]))
        y = yp if y is None else y + yp

    @pl.when(f == 0)
    def _():
        acc_ref[...] = y

    @pl.when(f > 0)
    def _():
        acc_ref[...] += y

    @pl.when(f == last)
    def _():
        vx = vx_ref[0]
        row = lax.broadcasted_iota(jnp.int32, (L, 1), 0)
        sgn = jnp.where(jnp.bitwise_and(row, 1) == 0, 1.0, -1.0)
        u_ny = jnp.sum(sgn * vx, axis=0, keepdims=True)
        yy = acc_ref[...] + sgn * (u_ny * kny_ref[...])
        o_ref[0] = ((yy + skip_ref[...] * vx) * x0_ref[0]).astype(o_ref.dtype)


def _prep_hyena(w_in, b_in, w_short, b_short, f_w1, f_b1, f_w2, f_b2, f_w3, sin_freq, skip, w_out, b_out):
    D = w_in.shape[0]
    tc = CH_TILE
    nc = D // tc
    winr = w_in.reshape(D, 3, nc, tc).transpose(2, 0, 1, 3).reshape(nc, D, 3 * tc).astype(BF16)
    binr = b_in.reshape(3, nc, tc).transpose(1, 0, 2).reshape(nc, 1, 3 * tc)
    wshr = w_short.reshape(3, 3, nc, tc).transpose(2, 0, 1, 3).reshape(nc, 3, 3 * tc)
    bshr = b_short.reshape(3, nc, tc).transpose(1, 0, 2).reshape(nc, 1, 3 * tc)
    fw1 = jnp.concatenate([f_w1, jnp.zeros((LANES - POS_EMB, FILTER_FO), F32)], axis=0).astype(BF16)
    deltas = jnp.linspace(math.log(DECAY_TARGET) / DECAY_FAST, math.log(DECAY_TARGET) / DECAY_SLOW, D,
                          dtype=F32).reshape(1, D)
    return (winr, binr, wshr, bshr, fw1, f_b1.reshape(1, -1), f_w2.astype(BF16), f_b2.reshape(1, -1),
            f_w3.astype(BF16), sin_freq.reshape(1, -1), deltas, skip.reshape(1, D), w_out.astype(BF16),
            b_out.reshape(1, D))


def _hyena_filter(L, hw):
    (_, _, _, _, fw1, fb1, fw2, fb2, fw3, fr, deltas, _, _, _) = hw
    D = deltas.shape[1]
    tc = CH_TILE
    nc = D // tc
    t, z = _filter_features(L)
    cos_t, sin_t = _dft_tables(L)
    const = lambda c: (0, 0)
    kc, ks, kny = pl.pallas_call(
        functools.partial(_hy_filter_kernel, L=L),
        out_shape=(jax.ShapeDtypeStruct((L, D), F32), jax.ShapeDtypeStruct((L, D), F32),
                   jax.ShapeDtypeStruct((1, D), F32)),
        grid=(nc,),
        in_specs=[
            pl.BlockSpec(z.shape, const),
            pl.BlockSpec(t.shape, const),
            pl.BlockSpec((1, tc), lambda c: (0, c)),
            pl.BlockSpec(fw1.shape, const),
            pl.BlockSpec(fb1.shape, const),
            pl.BlockSpec(fw2.shape, const),
            pl.BlockSpec(fb2.shape, const),
            pl.BlockSpec((FILTER_FO, tc), lambda c: (0, c)),
            pl.BlockSpec((FILTER_FO, tc), lambda c: (0, nc + c)),
            pl.BlockSpec(fr.shape, const),
            pl.BlockSpec((L, L), const),
            pl.BlockSpec((L, L), const),
        ],
        out_specs=(pl.BlockSpec((L, tc), lambda c: (0, c)), pl.BlockSpec((L, tc), lambda c: (0, c)),
                   pl.BlockSpec((1, tc), lambda c: (0, c))),
        compiler_params=_cparams(("arbitrary",), VMEM_BIG),
        name="hyena_filter",
    )(z, t, deltas, fw1, fb1, fw2, fb2, fw3, fw3, fr, cos_t, sin_t)
    return cos_t, sin_t, kc, ks, kny


LCONV_TC = 512
LCONV_TF = 512


def _hyena(x, mod, gain, hw, filt):
    (winr, binr, wshr, bshr, _, _, _, _, _, _, _, skip, w_out, b_out) = hw
    cos_t, sin_t, kc, ks, kny = filt
    B, L, D = x.shape
    tc = CH_TILE
    nc = D // tc
    mod_b = mod.shape[0] > 1
    R = min(CONF_ROWS, L)
    x0, vx = pl.pallas_call(
        functools.partial(_hy_in_kernel, L=L, R=R, tc=tc),
        out_shape=(jax.ShapeDtypeStruct((B, L, D), F32), jax.ShapeDtypeStruct((B, L, D), F32)),
        grid=(B, nc),
        in_specs=[
            pl.BlockSpec((1, L, D), lambda b, c: (b, 0, 0)),
            pl.BlockSpec((1, 6, D), (lambda b, c: (b, 0, 0)) if mod_b else (lambda b, c: (0, 0, 0))),
            pl.BlockSpec((1, D), lambda b, c: (0, 0)),
            pl.BlockSpec((1, D, 3 * tc), lambda b, c: (c, 0, 0)),
            pl.BlockSpec((1, 1, 3 * tc), lambda b, c: (c, 0, 0)),
            pl.BlockSpec((1, 3, 3 * tc), lambda b, c: (c, 0, 0)),
            pl.BlockSpec((1, 1, 3 * tc), lambda b, c: (c, 0, 0)),
        ],
        out_specs=(pl.BlockSpec((1, L, tc), lambda b, c: (b, 0, c)), pl.BlockSpec((1, L, tc), lambda b, c: (b, 0, c))),
        scratch_shapes=[pltpu.VMEM((L + 2 * CONV_PAD, D), BF16),
                        pltpu.VMEM((L // R, R + 2 * CONV_PAD, 3 * tc), F32)],
        compiler_params=_cparams(("parallel", "arbitrary"), VMEM_BIG),
        name="hyena_in_shortconv",
    )(x, mod, gain, winr, binr, wshr, bshr)

    tcl = min(LCONV_TC, D)
    tfq = min(LCONV_TF, L)
    y = pl.pallas_call(
        functools.partial(_hy_conv_kernel, L=L),
        out_shape=jax.ShapeDtypeStruct((B, L, D), BF16),
        grid=(B, D // tcl, L // tfq),
        in_specs=[
            pl.BlockSpec((1, L, tcl), lambda b, c, f: (b, 0, c)),
            pl.BlockSpec((1, L, tcl), lambda b, c, f: (b, 0, c)),
            pl.BlockSpec((tfq, L), lambda b, c, f: (f, 0)),
            pl.BlockSpec((tfq, L), lambda b, c, f: (f, 0)),
            pl.BlockSpec((L, tfq), lambda b, c, f: (0, f)),
            pl.BlockSpec((L, tfq), lambda b, c, f: (0, f)),
            pl.BlockSpec((tfq, tcl), lambda b, c, f: (f, c)),
            pl.BlockSpec((tfq, tcl), lambda b, c, f: (f, c)),
            pl.BlockSpec((1, tcl), lambda b, c, f: (0, c)),
            pl.BlockSpec((1, tcl), lambda b, c, f: (0, c)),
        ],
        out_specs=pl.BlockSpec((1, L, tcl), lambda b, c, f: (b, 0, c)),
        scratch_shapes=[pltpu.VMEM((L, tcl), BF16), pltpu.VMEM((L, tcl), F32)],
        compiler_params=_cparams(("parallel", "parallel", "arbitrary"), VMEM_BIG),
        name="hyena_longconv",
    )(vx, x0, cos_t, sin_t, cos_t, sin_t, kc, ks, kny, skip)
    return y, w_out, b_out


def kernel(x, c, ctx, c_ctx, ada_w, ada_b, norm_mix, norm_ffn, mla_w_dq, mla_q_norm, mla_w_uq, mla_w_dkv, mla_kv_norm, mla_w_ukv, mla_qk_gain, mla_w_o, cf_w_pw1, cf_b_pw1, cf_w_dw, cf_b_dw, cf_ln_g, cf_ln_b, cf_w_pw2, cf_b_pw2, hy_w_in, hy_b_in, hy_w_short, hy_b_short, hy_f_w1, hy_f_b1, hy_f_w2, hy_f_b2, hy_f_w3, hy_sin_freq, hy_skip, hy_w_out, hy_b_out, ffn_w_up, ffn_w_dw, ffn_b_dw, ffn_w_down):
    B, L, D = x.shape
    Lc = ctx.shape[1]
    depth = ada_w.shape[0]

    rows = ((B + 1 + 7) // 8) * 8
    cvec = jnp.concatenate([c, c_ctx[None, :], jnp.zeros((rows - B - 1, D), F32)], axis=0)
    ada = _ada_all(cvec, ada_w, ada_b)
    zero_bias = jnp.zeros((1, D), F32)
    rope = None
    fw = _prep_ffn(ffn_w_up, ffn_w_dw, ffn_b_dw, ffn_w_down)

    for i in range(depth):
        kind = i % N_MIXERS
        j = i // N_MIXERS
        need_ctx_out = i < depth - 1
        modl = ada[i, :B].reshape(B, 6, D)
        modc = ada[i, B:B + 1].reshape(1, 6, D)
        gmix = norm_mix[i].reshape(1, D)
        gffn = norm_ffn[i].reshape(1, D)
        if kind == 0:
            mw = _prep_mla(mla_w_dq[j], mla_q_norm[j], mla_w_uq[j], mla_w_dkv[j], mla_kv_norm[j], mla_w_ukv[j],
                           mla_qk_gain[j], mla_w_o[j])
            if rope is None:
                rope = _rope_table(Lc, L)
            q, k, vt = _mla_prep(ctx, x, modl, modc, gmix, mw, rope)
            mix_l = (_attention(q, k, vt, 0, L, min(ATTN_TQ, L), 0, Lc + L), mw[-1], zero_bias)
            if need_ctx_out:
                mix_c = (_attention(q, k, vt, L // Lc, Lc, Lc, L // Lc, Lc), mw[-1], zero_bias)
        elif kind == 1:
            cw = _prep_conf(cf_w_pw1[j], cf_b_pw1[j], cf_w_dw[j], cf_b_dw[j], cf_ln_g[j], cf_ln_b[j], cf_w_pw2[j],
                            cf_b_pw2[j])
            x = _conformer(x, modl, gmix, cw)
            mix_l = None
            if need_ctx_out:
                ctx = _conformer(ctx, modc, gmix, cw)
                mix_c = None
        else:
            hw = _prep_hyena(hy_w_in[j], hy_b_in[j], hy_w_short[j], hy_b_short[j], hy_f_w1[j], hy_f_b1[j],
                             hy_f_w2[j], hy_f_b2[j], hy_f_w3[j], hy_sin_freq[j], hy_skip[j], hy_w_out[j],
                             hy_b_out[j])
            mix_l = _hyena(x, modl, gmix, hw, _hyena_filter(L, hw))
            if need_ctx_out:
                mix_c = _hyena(ctx, modc, gmix, hw, _hyena_filter(Lc, hw))
        x = _ffn(x, modl, gffn, fw, i, mix_l)
        if need_ctx_out:
            ctx = _ffn(ctx, modc, gffn, fw, i, mix_c)
    return x
```

```python
import functools
import math

import numpy as np
import jax
import jax.numpy as jnp
from jax import lax
from jax.experimental import pallas as pl
from jax.experimental.pallas import tpu as pltpu

F32 = jnp.float32
BF16 = jnp.bfloat16

D_MODEL = 1024
DEPTH = 4
GRID_W = 64
N_MIXERS = 3
EPS = 1e-6
N_HEADS = 16
QK_NOPE = 64
QK_ROPE = 32
V_DIM = 64
Q_LORA = 256
KV_LORA = 128
ROPE_THETA = 10000.0
CONV_W = 31
POS_EMB = 33
FILTER_FO = 64
DECAY_FAST = 0.3
DECAY_SLOW = 1.5
DECAY_TARGET = 1e-2
D_FF = 2816

LANES = 128
HEAD_SLOT = 128
MIB = 1024 * 1024
VMEM_BIG = 58 * MIB
VMEM_MID = 48 * MIB


def _cparams(sem, vmem=VMEM_MID):
    return pltpu.CompilerParams(dimension_semantics=sem, vmem_limit_bytes=vmem)


def _sigmoid(x):
    return 1.0 / (1.0 + jnp.exp(-x))


def _silu(x):
    return x * _sigmoid(x)


def _rms_scale(x, n):
    return lax.rsqrt(jnp.sum(x * x, axis=-1, keepdims=True) * (1.0 / n) + EPS)


def _modulate(x, gain, shift, scale):
    y = x * _rms_scale(x, x.shape[-1]) * gain
    return y * (1.0 + scale) + shift


def _dot(a, b):
    return jnp.dot(a, b, preferred_element_type=F32)


def _ada_kernel(c_ref, w_ref, b_ref, o_ref):
    s = _silu(c_ref[...]).astype(BF16)
    o_ref[0] = _dot(s, w_ref[0].astype(BF16)) + b_ref[0]


def _ada_all(cvec, ada_w, ada_b):
    rows = cvec.shape[0]
    depth, d, n = ada_w.shape
    tn = 1536
    return pl.pallas_call(
        _ada_kernel,
        out_shape=jax.ShapeDtypeStruct((depth, rows, n), F32),
        grid=(depth, n // tn),
        in_specs=[
            pl.BlockSpec((rows, d), lambda i, j: (0, 0)),
            pl.BlockSpec((1, d, tn), lambda i, j: (i, 0, j)),
            pl.BlockSpec((1, 1, tn), lambda i, j: (i, 0, j)),
        ],
        out_specs=pl.BlockSpec((1, rows, tn), lambda i, j: (i, 0, j)),
        compiler_params=_cparams(("parallel", "parallel")),
        name="ada_mod",
    )(cvec, ada_w, ada_b.reshape(depth, 1, n))


FFN_TILE = 256


FFN_HALO = 16
FFN_ROWS = 512


def _ffn_kernel(*refs, R, tf, whole_seq, mixer_proj, mixer_ln):
    if mixer_proj:
        (xp_ref, x_ref, xn_ref, ap_ref, a_ref, an_ref, wp_ref, bp_ref, lng_ref, lnb_ref, mod_ref, gain_ref, wup_ref,
         wdw_ref, bdw_ref, wdn_ref, o_ref, h_ref, pad_ref, act_ref, acat_ref, xin_ref) = refs
    else:
        (xp_ref, x_ref, xn_ref, mod_ref, gain_ref, wup_ref, wdw_ref, bdw_ref, wdn_ref, o_ref,
         h_ref, pad_ref, act_ref) = refs
        xin_ref = None
    c = pl.program_id(1)
    nc = pl.num_programs(1)
    H = FFN_HALO
    D = x_ref.shape[-1]
    nj = D_FF // tf
    m = mod_ref[0]
    gain = gain_ref[...]

    def modulated(v):
        return _modulate(v, gain, m[3:4], m[4:5])

    def mixer_rows(v):
        if not mixer_ln:
            return v
        mu = jnp.mean(v, axis=-1, keepdims=True)
        vc = v - mu
        var = jnp.mean(vc * vc, axis=-1, keepdims=True)
        return _silu(vc * lax.rsqrt(var + EPS) * lng_ref[...] + lnb_ref[...]).astype(BF16)

    x_main, x_prev, x_next = x_ref[0], None, None
    if mixer_proj and whole_seq:
        x_main = x_main + m[2:3] * (_dot(mixer_rows(a_ref[0]), wp_ref[...]) + bp_ref[...])
    elif mixer_proj:
        acat_ref[0:H, :] = mixer_rows(ap_ref[0])
        acat_ref[H:H + R, :] = mixer_rows(a_ref[0])
        acat_ref[H + R:2 * H + R, :] = mixer_rows(an_ref[0])
        proj = m[2:3] * (_dot(acat_ref[...], wp_ref[...]) + bp_ref[...])
        x_main = x_main + proj[H:H + R]
        x_prev = xp_ref[0] + proj[0:H]
        x_next = xn_ref[0] + proj[H + R:2 * H + R]
    elif not whole_seq:
        x_prev, x_next = xp_ref[0], xn_ref[0]
    if mixer_proj:
        xin_ref[...] = x_main

    h_ref[H:H + R, :] = modulated(x_main).astype(BF16)
    if whole_seq:
        h_ref[0:H, :] = jnp.zeros((H, D), BF16)
        h_ref[H + R:2 * H + R, :] = jnp.zeros((H, D), BF16)
    else:
        h_ref[0:H, :] = jnp.where(c > 0, modulated(x_prev), 0.0).astype(BF16)
        h_ref[H + R:2 * H + R, :] = jnp.where(c < nc - 1, modulated(x_next), 0.0).astype(BF16)

    def up(t):
        cols = slice(t * tf, (t + 1) * tf)
        pad_ref[t] = _dot(h_ref[...], wup_ref[0, :, cols])
        return _dot(h_ref[H:H + R, :], wup_ref[0, :, D_FF + t * tf:D_FF + (t + 1) * tf])

    def gate(t, lin):
        cols = slice(t * tf, (t + 1) * tf)
        w = wdw_ref[0, :, cols]
        g = (pad_ref[t, H - 1:H - 1 + R, :] * w[0:1] + pad_ref[t, H:H + R, :] * w[1:2]
             + pad_ref[t, H + 1:H + 1 + R, :] * w[2:3] + bdw_ref[0, :, cols])
        act_ref[:, cols] = (_silu(g) * lin).astype(BF16)

    lin = {0: up(0)}
    for t in range(nj):
        if t + 1 < nj:
            lin[t + 1] = up(t + 1)
        gate(t, lin.pop(t))
    y = _dot(act_ref[...], wdn_ref[0])
    o_ref[0] = (xin_ref[...] if mixer_proj else x_ref[0]) + m[5:6] * y


def _prep_ffn(w_up, w_dw, b_dw, w_down):
    depth = w_up.shape[0]
    return w_up.astype(BF16), w_dw, b_dw.reshape(depth, 1, D_FF), w_down.astype(BF16)


def _ffn(x, mod, gain, ffn_w, layer, mixer=None):
    wup, wdw, bdw, wdn = ffn_w
    B, L, D = x.shape
    tf = FFN_TILE
    nj = D_FF // tf
    mod_b = mod.shape[0] > 1
    R = min(FFN_ROWS, L)
    H = FFN_HALO
    hb = R // H
    last_hb = L // H - 1
    resident = pl.Buffered(1)

    def row_specs(width):
        return [pl.BlockSpec((1, H, width), lambda b, c: (b, jnp.maximum(c * hb - 1, 0), 0)),
                pl.BlockSpec((1, R, width), lambda b, c: (b, c, 0)),
                pl.BlockSpec((1, H, width), lambda b, c: (b, jnp.minimum((c + 1) * hb, last_hb), 0))]

    in_specs = row_specs(D)
    args = [x, x, x]
    scratch = [pltpu.VMEM((R + 2 * H, D), BF16), pltpu.VMEM((nj, R + 2 * H, tf), F32), pltpu.VMEM((R, D_FF), BF16)]
    mixer_ln = False
    if mixer is not None:
        a, wp, bp, ln = mixer
        K = a.shape[-1]
        mixer_ln = ln is not None
        lng, lnb = ln if mixer_ln else (jnp.ones((1, K), F32), jnp.zeros((1, K), F32))
        in_specs += row_specs(K) + [pl.BlockSpec((K, D), lambda b, c: (0, 0), pipeline_mode=resident),
                                    pl.BlockSpec((1, D), lambda b, c: (0, 0)),
                                    pl.BlockSpec((1, K), lambda b, c: (0, 0)),
                                    pl.BlockSpec((1, K), lambda b, c: (0, 0))]
        args += [a, a, a, wp, bp, lng, lnb]
        scratch += [pltpu.VMEM((R + 2 * H, K), BF16), pltpu.VMEM((R, D), F32)]
    in_specs += [
        pl.BlockSpec((1, 6, D), (lambda b, c: (b, 0, 0)) if mod_b else (lambda b, c: (0, 0, 0))),
        pl.BlockSpec((1, D), lambda b, c: (0, 0)),
        pl.BlockSpec((1, D, 2 * D_FF), lambda b, c: (layer, 0, 0), pipeline_mode=resident),
        pl.BlockSpec((1, 3, D_FF), lambda b, c: (layer, 0, 0)),
        pl.BlockSpec((1, 1, D_FF), lambda b, c: (layer, 0, 0)),
        pl.BlockSpec((1, D_FF, D), lambda b, c: (layer, 0, 0), pipeline_mode=resident),
    ]
    args += [mod, gain, wup, wdw, bdw, wdn]
    return pl.pallas_call(
        functools.partial(_ffn_kernel, R=R, tf=tf, whole_seq=(R == L), mixer_proj=mixer is not None,
                          mixer_ln=mixer_ln),
        out_shape=jax.ShapeDtypeStruct((B, L, D), F32),
        grid=(B, L // R),
        in_specs=in_specs,
        out_specs=pl.BlockSpec((1, R, D), lambda b, c: (b, c, 0)),
        scratch_shapes=scratch,
        compiler_params=_cparams(("parallel", "parallel"), VMEM_BIG),
        name="conv_ffn",
    )(*args)


MLA_TL = 256
MLA_SUB = 128


def _rope_partner(j):
    return j + 8 if (j % 16) < 8 else j - 8


def _rope_table(n_ctx, L):
    rows = L // GRID_W
    row = jnp.repeat(jnp.arange(rows), GRID_W).astype(F32)
    col = jnp.tile(jnp.arange(GRID_W), rows).astype(F32)
    half = QK_ROPE // 2
    inv = ROPE_THETA ** (-(jnp.arange(0, half, 2, dtype=F32) / half))
    ang = jnp.concatenate([row[:, None] * inv, col[:, None] * inv], axis=-1)
    ang = jnp.concatenate([ang, jnp.zeros((n_ctx, QK_ROPE // 2), F32)], axis=0)
    idx = jnp.array([(j // 16) * 8 + (j % 8) for j in range(QK_ROPE)])
    sgn = jnp.array([-1.0 if (j % 16) < 8 else 1.0 for j in range(QK_ROPE)], F32)
    a = ang[:, idx]
    ones = jnp.ones((ang.shape[0], QK_NOPE), F32)
    return jnp.concatenate([ones, jnp.cos(a), jnp.sin(a) * sgn], axis=-1)


def _prep_mla(w_dq, q_norm, w_uq, w_dkv, kv_norm, w_ukv, qk_gain, w_o):
    H = N_HEADS
    perm = jnp.array([_rope_partner(j) for j in range(QK_ROPE)])
    D = w_dq.shape[0]
    w_kpe = w_dkv[:, KV_LORA:]
    wd = jnp.concatenate([w_dq, w_dkv[:, :KV_LORA], jnp.zeros((D, QK_NOPE), F32), w_kpe, w_kpe[:, perm]],
                         axis=1).astype(BF16)
    wq = w_uq.reshape(Q_LORA, H, QK_NOPE + QK_ROPE)
    wq_pe = wq[:, :, QK_NOPE:]
    wuq = jnp.concatenate([wq, wq_pe[:, :, perm]], axis=-1).reshape(Q_LORA, H * HEAD_SLOT).astype(BF16)
    wkv = w_ukv.reshape(KV_LORA, H, QK_NOPE + V_DIM)
    wk = jnp.concatenate([wkv[:, :, :QK_NOPE], jnp.zeros((KV_LORA, H, HEAD_SLOT - QK_NOPE), F32)], axis=-1)
    wuk = wk.reshape(KV_LORA, H * HEAD_SLOT).astype(BF16)
    wuv_t = wkv[:, :, QK_NOPE:].reshape(KV_LORA, H * V_DIM).T.astype(BF16)
    scale = (QK_NOPE + QK_ROPE) ** -0.5 * math.log2(math.e)
    cn, cp = math.sqrt(QK_NOPE), math.sqrt(QK_ROPE)
    gq = qk_gain[0]
    gq_slot = jnp.concatenate([gq[:QK_NOPE] * cn, gq[QK_NOPE:] * cp, gq[QK_NOPE:][perm] * cp]) * scale
    gk = qk_gain[1]
    gk_pe_slot = jnp.concatenate([jnp.zeros((QK_NOPE,), F32), gk[QK_NOPE:], gk[QK_NOPE:][perm]]) * cp
    gk_n_slot = jnp.concatenate([gk[:QK_NOPE] * cn, jnp.zeros((HEAD_SLOT - QK_NOPE,), F32)])
    gains = jnp.stack([gq_slot, gk_pe_slot, gk_n_slot], axis=0)
    return (wd, q_norm.reshape(1, Q_LORA), kv_norm.reshape(1, KV_LORA), wuq, wuk, wuv_t, gains, w_o.astype(BF16))


def _mla_prep_kernel(ctx_ref, x_ref, modl_ref, modc_ref, gain_ref, wd_ref, qn_ref, kvn_ref, wuq_ref, wuk_ref,
                     wuvt_ref, gkn_ref, tab_ref, q_ref, k_ref, vt_ref):
    is_ctx = pl.program_id(1) == pl.num_programs(1) - 1
    m = jnp.where(is_ctx, modc_ref[0], modl_ref[0])
    lane = lax.broadcasted_iota(jnp.int32, (1, HEAD_SLOT), 1)
    nope_mask = lane < QK_NOPE
    pe_mask = jnp.logical_and(lane >= QK_NOPE, lane < QK_NOPE + QK_ROPE)
    g_kn = gkn_ref[...]
    fold = HEAD_SLOT - QK_ROPE
    tl = x_ref.shape[1]
    parts = tl // MLA_SUB

    def sumsq(v):
        return jnp.sum(v * v, axis=-1, keepdims=True)

    def project(p):
        rs = slice(p * MLA_SUB, (p + 1) * MLA_SUB)
        xin = jnp.where(is_ctx, ctx_ref[0, rs, :], x_ref[0, rs, :])
        h = _modulate(xin, gain_ref[...], m[0:1], m[1:2]).astype(BF16)
        d = _dot(h, wd_ref[...])
        cq = d[:, :Q_LORA]
        cq = (cq * _rms_scale(cq, Q_LORA) * qn_ref[...]).astype(BF16)
        ckv = d[:, Q_LORA:Q_LORA + KV_LORA]
        ckv = (ckv * _rms_scale(ckv, KV_LORA) * kvn_ref[...]).astype(BF16)
        kp = d[:, Q_LORA + KV_LORA:]
        xk = kp * lax.rsqrt(0.5 * sumsq(kp) + QK_ROPE * EPS)
        kpe_slot = (xk * tab_ref[rs, 2 * HEAD_SLOT:3 * HEAD_SLOT]
                    + pltpu.roll(xk * tab_ref[rs, 3 * HEAD_SLOT:4 * HEAD_SLOT], fold, 1))
        qraw = _dot(cq, wuq_ref[...])
        kvu = _dot(ckv, wuk_ref[...])
        vt_ref[0, :, rs] = lax.dot_general(wuvt_ref[...], ckv, (((1,), (1,)), ((), ())),
                                           preferred_element_type=F32).astype(BF16)
        return qraw, kvu, kpe_slot

    def heads(p, qraw, kvu, kpe_slot):
        rs = slice(p * MLA_SUB, (p + 1) * MLA_SUB)
        tq_a = tab_ref[rs, 0 * HEAD_SLOT:1 * HEAD_SLOT]
        tq_b = tab_ref[rs, 1 * HEAD_SLOT:2 * HEAD_SLOT]

        def norms(hd):
            sl = slice(hd * HEAD_SLOT, (hd + 1) * HEAD_SLOT)
            sq = qraw[:, sl] * qraw[:, sl]
            rn = lax.rsqrt(jnp.sum(jnp.where(nope_mask, sq, 0.0), axis=-1, keepdims=True) + QK_NOPE * EPS)
            rp = lax.rsqrt(jnp.sum(jnp.where(pe_mask, sq, 0.0), axis=-1, keepdims=True) + QK_ROPE * EPS)
            rk = lax.rsqrt(sumsq(kvu[:, sl]) + QK_NOPE * EPS)
            return rn, rp, rk

        def emit(hd, rn, rp, rk):
            sl = slice(hd * HEAD_SLOT, (hd + 1) * HEAD_SLOT)
            xq = qraw[:, sl] * jnp.where(nope_mask, rn, rp)
            q_ref[0, rs, sl] = (xq * tq_a + pltpu.roll(xq * tq_b, fold, 1)).astype(BF16)
            k_ref[0, rs, sl] = (kvu[:, sl] * rk * g_kn + kpe_slot).astype(BF16)

        ahead = 2
        stats = {hd: norms(hd) for hd in range(min(ahead, N_HEADS))}
        for hd in range(N_HEADS):
            if hd + ahead < N_HEADS:
                stats[hd + ahead] = norms(hd + ahead)
            emit(hd, *stats.pop(hd))

    proj = {0: project(0)}
    for p in range(parts):
        if p + 1 < parts:
            proj[p + 1] = project(p + 1)
        heads(p, *proj.pop(p))


def _mla_prep(ctx, x, modl, modc, gain, mw, rope):
    wd, qn, kvn, wuq, wuk, wuv_t, gains, _ = mw
    B, L, D = x.shape
    Lc = ctx.shape[1]
    tl = MLA_TL
    assert Lc == tl and L % tl == 0
    nt = (Lc + L) // tl
    Lt = Lc + L
    H = N_HEADS
    const = lambda b, t: (0, 0)
    partner_lanes = jnp.arange(HEAD_SLOT) >= QK_NOPE + QK_ROPE
    tq = rope * gains[0:1]
    tk = rope * gains[1:2]
    tabs = jnp.concatenate([jnp.where(partner_lanes, 0.0, tq), jnp.where(partner_lanes, tq, 0.0),
                            jnp.where(partner_lanes, 0.0, tk), jnp.where(partner_lanes, tk, 0.0)], axis=1)
    gkn = gains[2:3]
    return pl.pallas_call(
        _mla_prep_kernel,
        out_shape=(jax.ShapeDtypeStruct((B, Lt, H * HEAD_SLOT), BF16),
                   jax.ShapeDtypeStruct((B, Lt, H * HEAD_SLOT), BF16),
                   jax.ShapeDtypeStruct((B, H * V_DIM, Lt), BF16)),
        grid=(B, nt),
        in_specs=[
            pl.BlockSpec((1, tl, D), lambda b, t: (b, 0, 0)),
            pl.BlockSpec((1, tl, D), lambda b, t: (b, jnp.minimum(t, nt - 2), 0)),
            pl.BlockSpec((1, 6, D), lambda b, t: (b, 0, 0)),
            pl.BlockSpec((1, 6, D), lambda b, t: (0, 0, 0)),
            pl.BlockSpec((1, D), const),
            pl.BlockSpec(wd.shape, const),
            pl.BlockSpec(qn.shape, const),
            pl.BlockSpec(kvn.shape, const),
            pl.BlockSpec(wuq.shape, const),
            pl.BlockSpec(wuk.shape, const),
            pl.BlockSpec(wuv_t.shape, const),
            pl.BlockSpec(gkn.shape, const),
            pl.BlockSpec((tl, 4 * HEAD_SLOT), lambda b, t: (t, 0)),
        ],
        out_specs=(pl.BlockSpec((1, tl, H * HEAD_SLOT), lambda b, t: (b, t, 0)),
                   pl.BlockSpec((1, tl, H * HEAD_SLOT), lambda b, t: (b, t, 0)),
                   pl.BlockSpec((1, H * V_DIM, tl), lambda b, t: (b, 0, t))),
        compiler_params=_cparams(("parallel", "parallel")),
        name="mla_prep",
    )(ctx, x, modl, modc, gain, wd, qn, kvn, wuq, wuk, wuv_t, gkn, tabs)


HEADS_PER_STEP = 2
ATTN_TQ = 2048


ATTN_QCOLS = 256
ATTN_KEY_BLOCK = 128


def _attn_kernel(q_ref, k_ref, vt_ref, o_ref, st_ref, p_ref):
    tq = q_ref.shape[1]
    heads = q_ref.shape[2] // HEAD_SLOT
    chains = [(qc, hh) for qc in range(tq // ATTN_QCOLS) for hh in range(heads)]
    n = len(chains)

    nk = k_ref.shape[1]
    s_slots = st_ref.shape[0]
    p_slots = p_ref.shape[0]
    kb = ATTN_KEY_BLOCK
    grp = (kb // 8, 8, ATTN_QCOLS)

    def scores(i):
        qc, hh = chains[i]
        sl = slice(hh * HEAD_SLOT, (hh + 1) * HEAD_SLOT)
        st_ref[i % s_slots] = lax.dot_general(
            k_ref[0, :, sl], q_ref[0, qc * ATTN_QCOLS:(qc + 1) * ATTN_QCOLS, sl],
            (((1,), (1,)), ((), ())), preferred_element_type=F32)

    def softmax(i):
        s_slot, p_slot = i % s_slots, i % p_slots
        mx = jnp.full(grp[1:], -jnp.inf, F32)
        for r0 in range(0, nk, kb):
            mx = jnp.maximum(mx, jnp.max(st_ref[s_slot, r0:r0 + kb, :].reshape(grp), axis=0))
        m = jnp.max(mx, axis=0, keepdims=True)
        acc = jnp.zeros(grp[1:], F32)
        for r0 in range(0, nk, kb):
            p = jnp.exp2(st_ref[s_slot, r0:r0 + kb, :] - m)
            acc = acc + jnp.sum(p.reshape(grp), axis=0)
            p_ref[p_slot, r0:r0 + kb, :] = p.astype(BF16)
        return jnp.sum(acc, axis=0, keepdims=True)

    def weighted_values(i, l):
        hh = chains[i][1]
        return _dot(vt_ref[0, hh * V_DIM:(hh + 1) * V_DIM, :], p_ref[i % p_slots]) / l

    scores(0)
    if n > 1:
        scores(1)
    sm = {0: softmax(0)}
    done = {}
    for i in range(n):
        if i + 2 < n:
            scores(i + 2)
        done[chains[i]] = weighted_values(i, sm.pop(i))
        if i + 1 < n:
            sm[i + 1] = softmax(i + 1)
        qc, hh = chains[i]
        if hh % 2 == 1:
            ot = jnp.concatenate([done.pop((qc, hh - 1)), done.pop((qc, hh))], axis=0)
            o_ref[0, qc * ATTN_QCOLS:(qc + 1) * ATTN_QCOLS, (hh - 1) * V_DIM:(hh + 1) * V_DIM] = (
                ot.T.astype(o_ref.dtype))


def _attention(q, k, vt, q_tile0, n_q, tq, k_tile0, n_k, heads_per_step=HEADS_PER_STEP):
    B = q.shape[0]
    H = N_HEADS
    nhp = H // heads_per_step
    wq = heads_per_step * HEAD_SLOT
    wv = heads_per_step * V_DIM
    return pl.pallas_call(
        _attn_kernel,
        out_shape=jax.ShapeDtypeStruct((B, n_q, H * V_DIM), BF16),
        grid=(B, nhp, n_q // tq),
        in_specs=[
            pl.BlockSpec((1, tq, wq), lambda b, h, t: (b, q_tile0 + t, h)),
            pl.BlockSpec((1, n_k, wq), lambda b, h, t: (b, k_tile0, h)),
            pl.BlockSpec((1, wv, n_k), lambda b, h, t: (b, h, k_tile0)),
        ],
        out_specs=pl.BlockSpec((1, tq, wv), lambda b, h, t: (b, t, h)),
        scratch_shapes=[pltpu.VMEM((3, n_k, ATTN_QCOLS), F32), pltpu.VMEM((2, n_k, ATTN_QCOLS), BF16)],
        compiler_params=_cparams(("parallel", "parallel", "arbitrary")),
        name="mla_attn",
    )(q, k, vt)


CH_TILE = 256
CONV_PAD = 16
CONV_ROWS = 128
CONF_ROWS = 512


def _conf_a_kernel(x_ref, mod_ref, gain_ref, w1_ref, b1_ref, wdw_ref, bdw_ref, o_ref, h_ref, pad_ref,
                   *, L, R, tc):
    c = pl.program_id(1)
    H = CONV_PAD
    D = x_ref.shape[-1]
    n = L // R

    @pl.when(c == 0)
    def _():
        m = mod_ref[0]
        h_ref[0:H, :] = jnp.zeros((H, D), BF16)
        h_ref[H:H + L, :] = _modulate(x_ref[0], gain_ref[...], m[0:1], m[1:2]).astype(BF16)
        h_ref[H + L:2 * H + L, :] = jnp.zeros((H, D), BF16)

    w = wdw_ref[...]
    bias = bdw_ref[...]
    half = (CONV_W - 1) // 2
    rows = min(CONV_ROWS, R)
    win = rows + 2 * H

    def project(i):
        return _dot(h_ref[i * R:i * R + R + 2 * H, :], w1_ref[0]) + b1_ref[0]

    def glu_conv(i, a):
        pad_ref[i] = a[:, :tc] * _sigmoid(a[:, tc:])
        if i == 0:
            pad_ref[i, 0:H, :] = jnp.zeros((H, tc), F32)
        if i == n - 1:
            pad_ref[i, H + R:2 * H + R, :] = jnp.zeros((H, tc), F32)
        for q0 in range(0, R, rows):
            for l0 in range(0, tc, LANES):
                lanes = slice(l0, l0 + LANES)
                window = pad_ref[i, q0:q0 + win, lanes]
                acc = jnp.broadcast_to(bias[:, lanes], (rows, LANES))
                for r in range(8):
                    rot = window if r == 0 else pltpu.roll(window, win - r, 0)
                    for k in range(CONV_W):
                        off = k + 1 + (H - 1 - half)
                        if off % 8 == r:
                            acc = acc + rot[off - r:off - r + rows, :] * w[k:k + 1, lanes]
                o_ref[0, i * R + q0:i * R + q0 + rows, lanes] = acc

    a = {0: project(0)}
    for i in range(n):
        if i + 1 < n:
            a[i + 1] = project(i + 1)
        glu_conv(i, a.pop(i))


def _prep_conf(w1, b1, wdw, bdw, lng, lnb, w2, b2):
    D = w1.shape[0]
    tc = CH_TILE
    nc = D // tc
    w1r = w1.reshape(D, 2, nc, tc).transpose(2, 0, 1, 3).reshape(nc, D, 2 * tc).astype(BF16)
    b1r = b1.reshape(2, nc, tc).transpose(1, 0, 2).reshape(nc, 1, 2 * tc)
    return (w1r, b1r, wdw, bdw.reshape(1, D), lng.reshape(1, D), lnb.reshape(1, D), w2.astype(BF16),
            b2.reshape(1, D))


def _conformer(x, mod, gain, cw):
    w1r, b1r, wdw, bdw, lng, lnb, w2, b2 = cw
    B, L, D = x.shape
    tc = CH_TILE
    nc = D // tc
    mod_b = mod.shape[0] > 1
    mod_map2 = (lambda b, c: (b, 0, 0)) if mod_b else (lambda b, c: (0, 0, 0))
    R = min(CONF_ROWS, L)
    u = pl.pallas_call(
        functools.partial(_conf_a_kernel, L=L, R=R, tc=tc),
        out_shape=jax.ShapeDtypeStruct((B, L, D), F32),
        grid=(B, nc),
        in_specs=[
            pl.BlockSpec((1, L, D), lambda b, c: (b, 0, 0)),
            pl.BlockSpec((1, 6, D), mod_map2),
            pl.BlockSpec((1, D), lambda b, c: (0, 0)),
            pl.BlockSpec((1, D, 2 * tc), lambda b, c: (c, 0, 0)),
            pl.BlockSpec((1, 1, 2 * tc), lambda b, c: (c, 0, 0)),
            pl.BlockSpec((CONV_W, tc), lambda b, c: (0, c)),
            pl.BlockSpec((1, tc), lambda b, c: (0, c)),
        ],
        out_specs=pl.BlockSpec((1, L, tc), lambda b, c: (b, 0, c)),
        scratch_shapes=[pltpu.VMEM((L + 2 * CONV_PAD, D), BF16),
                        pltpu.VMEM((L // R, R + 2 * CONV_PAD, tc), F32)],
        compiler_params=_cparams(("parallel", "arbitrary")),
        name="conformer_glu_dwconv",
    )(x, mod, gain, w1r, b1r, wdw, bdw)
    return u, w2, b2, (lng, lnb)


@functools.lru_cache(maxsize=None)
def _dft_tables_host(L):
    idx = np.arange(L, dtype=np.int64)
    ang = ((idx[:, None] * idx[None, :]) % (2 * L)).astype(np.float64) * (np.pi / L)
    return np.cos(ang).astype(BF16), np.sin(ang).astype(BF16)


def _dft_tables(L):
    cos_t, sin_t = _dft_tables_host(L)
    return jnp.asarray(cos_t), jnp.asarray(sin_t)


def _filter_features(L):
    t = jnp.linspace(0.0, 1.0, L, dtype=F32)[:, None]
    bands = (POS_EMB - 1) // 2
    w = 2.0 * math.pi * jnp.arange(L, dtype=F32) / L
    f = jnp.linspace(1e-4, bands - 1, bands, dtype=F32)
    fw = w[:, None] * f[None, :]
    z = jnp.concatenate([t, jnp.cos(fw), -jnp.sin(fw)], axis=-1)
    z = jnp.concatenate([z, jnp.zeros((L, LANES - POS_EMB), F32)], axis=-1)
    return t, z.astype(BF16)


def _hy_filter_kernel(z_ref, t_ref, dl_ref, w1_ref, b1_ref, w2_ref, b2_ref, w3f_ref, w3b_ref, fr_ref,
                      cos_ref, sin_ref, kc_ref, ks_ref, kny_ref, *, L):
    fr = fr_ref[...]
    hdn = jnp.sin(fr * (_dot(z_ref[...], w1_ref[...]) + b1_ref[...]))
    hdn = jnp.sin(fr * (_dot(hdn.astype(BF16), w2_ref[...]) + b2_ref[...])).astype(BF16)
    decay = jnp.exp(-t_ref[...] * jnp.abs(dl_ref[...]))
    row = lax.broadcasted_iota(jnp.int32, (L, 1), 0)
    h_fwd = _dot(hdn, w3f_ref[...]) * decay
    h_bwd = jnp.where(row > 0, _dot(hdn, w3b_ref[...]) * decay, 0.0)
    nrm = (jnp.sum(jnp.abs(h_fwd), axis=0, keepdims=True) + jnp.sum(jnp.abs(h_bwd), axis=0, keepdims=True) + EPS)
    inv = 1.0 / nrm
    ksum = (h_fwd + h_bwd) * inv
    kdif = (h_fwd - h_bwd) * inv
    n = 2 * L
    wcol = jnp.where(row == 0, 1.0 / n, 2.0 / n)
    sgn = jnp.where(jnp.bitwise_and(row, 1) == 0, 1.0, -1.0)

    def split_dot(tab, kk):
        hi = kk.astype(BF16)
        lo = (kk - hi.astype(F32)).astype(BF16)
        return _dot(tab, hi) + _dot(tab, lo)

    kc_ref[...] = split_dot(cos_ref[...], ksum) * wcol
    ks_ref[...] = -split_dot(sin_ref[...], kdif) * wcol
    kny_ref[...] = jnp.sum(sgn * ksum, axis=0, keepdims=True) * (1.0 / n)


def _hy_in_kernel(x_ref, mod_ref, gain_ref, win_ref, bin_ref, wsh_ref, bsh_ref, x0_ref, vx_ref, h_ref, pad_ref,
                  *, L, R, tc):
    c = pl.program_id(1)
    H = CONV_PAD
    D = x_ref.shape[-1]
    n = L // R

    @pl.when(c == 0)
    def _():
        m = mod_ref[0]
        h_ref[0:H, :] = jnp.zeros((H, D), BF16)
        h_ref[H:H + L, :] = _modulate(x_ref[0], gain_ref[...], m[0:1], m[1:2]).astype(BF16)
        h_ref[H + L:2 * H + L, :] = jnp.zeros((H, D), BF16)

    w = wsh_ref[0]

    def project(i):
        return _dot(h_ref[i * R:i * R + R + 2 * H, :], win_ref[0]) + bin_ref[0]

    def short_conv(i, a):
        pad_ref[i] = a
        if i == 0:
            pad_ref[i, 0:H, :] = jnp.zeros((H, 3 * tc), F32)
        if i == n - 1:
            pad_ref[i, H + R:2 * H + R, :] = jnp.zeros((H, 3 * tc), F32)
        u = (pad_ref[i, H - 1:H - 1 + R, :] * w[0:1] + pad_ref[i, H:H + R, :] * w[1:2]
             + pad_ref[i, H + 1:H + 1 + R, :] * w[2:3] + bsh_ref[0])
        x0_ref[0, i * R:(i + 1) * R, :] = u[:, :tc]
        vx_ref[0, i * R:(i + 1) * R, :] = u[:, 2 * tc:] * u[:, tc:2 * tc]

    a = {0: project(0)}
    for i in range(n):
        if i + 1 < n:
            a[i + 1] = project(i + 1)
        short_conv(i, a.pop(i))


def _hy_conv_kernel(vx_ref, x0_ref, cr_ref, sr_ref, cc_ref, sc_ref, kc_ref, ks_ref, kny_ref, skip_ref, o_ref,
                    u_ref, acc_ref, *, L):
    f = pl.program_id(2)
    last = pl.num_programs(2) - 1

    @pl.when(f == 0)
    def _():
        u_ref[...] = vx_ref[0].astype(BF16)

    u = u_ref[...]
    tfq = cr_ref.shape[0]
    parts = 2 if tfq % 256 == 0 else 1
    hp = tfq // parts

    def forward(p):
        rows = slice(p * hp, (p + 1) * hp)
        return _dot(cr_ref[rows, :], u), _dot(sr_ref[rows, :], u)

    def spectral(p, pc, ps):
        rows = slice(p * hp, (p + 1) * hp)
        kc = kc_ref[rows, :]
        ks = ks_ref[rows, :]
        return (pc * kc + ps * ks).astype(BF16), (ps * kc - pc * ks).astype(BF16)

    def inverse(p, zc, zs):
        cols = slice(p * hp, (p + 1) * hp)
        return _dot(cc_ref[:, cols], zc) + _dot(sc_ref[:, cols], zs)

    fw = [forward(p) for p in range(parts)]
    y = None
    for p in range(parts):
        yp = inverse(p, *spectral(p, *fw[p]))
        y = yp if y is None else y + yp

    @pl.when(f == 0)
    def _():
        acc_ref[...] = y

    @pl.when(f > 0)
    def _():
        acc_ref[...] += y

    @pl.when(f == last)
    def _():
        vx = vx_ref[0]
        row = lax.broadcasted_iota(jnp.int32, (L, 1), 0)
        sgn = jnp.where(jnp.bitwise_and(row, 1) == 0, 1.0, -1.0)
        u_ny = jnp.sum(sgn * vx, axis=0, keepdims=True)
        yy = acc_ref[...] + sgn * (u_ny * kny_ref[...])
        o_ref[0] = ((yy + skip_ref[...] * vx) * x0_ref[0]).astype(o_ref.dtype)


def _prep_hyena(w_in, b_in, w_short, b_short, f_w1, f_b1, f_w2, f_b2, f_w3, sin_freq, skip, w_out, b_out):
    D = w_in.shape[0]
    tc = CH_TILE
    nc = D // tc
    winr = w_in.reshape(D, 3, nc, tc).transpose(2, 0, 1, 3).reshape(nc, D, 3 * tc).astype(BF16)
    binr = b_in.reshape(3, nc, tc).transpose(1, 0, 2).reshape(nc, 1, 3 * tc)
    wshr = w_short.reshape(3, 3, nc, tc).transpose(2, 0, 1, 3).reshape(nc, 3, 3 * tc)
    bshr = b_short.reshape(3, nc, tc).transpose(1, 0, 2).reshape(nc, 1, 3 * tc)
    fw1 = jnp.concatenate([f_w1, jnp.zeros((LANES - POS_EMB, FILTER_FO), F32)], axis=0).astype(BF16)
    deltas = jnp.linspace(math.log(DECAY_TARGET) / DECAY_FAST, math.log(DECAY_TARGET) / DECAY_SLOW, D,
                          dtype=F32).reshape(1, D)
    return (winr, binr, wshr, bshr, fw1, f_b1.reshape(1, -1), f_w2.astype(BF16), f_b2.reshape(1, -1),
            f_w3.astype(BF16), sin_freq.reshape(1, -1), deltas, skip.reshape(1, D), w_out.astype(BF16),
            b_out.reshape(1, D))


def _hyena_filter(L, hw):
    (_, _, _, _, fw1, fb1, fw2, fb2, fw3, fr, deltas, _, _, _) = hw
    D = deltas.shape[1]
    tc = CH_TILE
    nc = D // tc
    t, z = _filter_features(L)
    cos_t, sin_t = _dft_tables(L)
    const = lambda c: (0, 0)
    kc, ks, kny = pl.pallas_call(
        functools.partial(_hy_filter_kernel, L=L),
        out_shape=(jax.ShapeDtypeStruct((L, D), F32), jax.ShapeDtypeStruct((L, D), F32),
                   jax.ShapeDtypeStruct((1, D), F32)),
        grid=(nc,),
        in_specs=[
            pl.BlockSpec(z.shape, const),
            pl.BlockSpec(t.shape, const),
            pl.BlockSpec((1, tc), lambda c: (0, c)),
            pl.BlockSpec(fw1.shape, const),
            pl.BlockSpec(fb1.shape, const),
            pl.BlockSpec(fw2.shape, const),
            pl.BlockSpec(fb2.shape, const),
            pl.BlockSpec((FILTER_FO, tc), lambda c: (0, c)),
            pl.BlockSpec((FILTER_FO, tc), lambda c: (0, nc + c)),
            pl.BlockSpec(fr.shape, const),
            pl.BlockSpec((L, L), const),
            pl.BlockSpec((L, L), const),
        ],
        out_specs=(pl.BlockSpec((L, tc), lambda c: (0, c)), pl.BlockSpec((L, tc), lambda c: (0, c)),
                   pl.BlockSpec((1, tc), lambda c: (0, c))),
        compiler_params=_cparams(("arbitrary",), VMEM_BIG),
        name="hyena_filter",
    )(z, t, deltas, fw1, fb1, fw2, fb2, fw3, fw3, fr, cos_t, sin_t)
    return cos_t, sin_t, kc, ks, kny


LCONV_TC = 512
LCONV_TF = 512


def _hyena(x, mod, gain, hw, filt):
    (winr, binr, wshr, bshr, _, _, _, _, _, _, _, skip, w_out, b_out) = hw
    cos_t, sin_t, kc, ks, kny = filt
    B, L, D = x.shape
    tc = CH_TILE
    nc = D // tc
    mod_b = mod.shape[0] > 1
    R = min(CONF_ROWS, L)
    x0, vx = pl.pallas_call(
        functools.partial(_hy_in_kernel, L=L, R=R, tc=tc),
        out_shape=(jax.ShapeDtypeStruct((B, L, D), F32), jax.ShapeDtypeStruct((B, L, D), F32)),
        grid=(B, nc),
        in_specs=[
            pl.BlockSpec((1, L, D), lambda b, c: (b, 0, 0)),
            pl.BlockSpec((1, 6, D), (lambda b, c: (b, 0, 0)) if mod_b else (lambda b, c: (0, 0, 0))),
            pl.BlockSpec((1, D), lambda b, c: (0, 0)),
            pl.BlockSpec((1, D, 3 * tc), lambda b, c: (c, 0, 0)),
            pl.BlockSpec((1, 1, 3 * tc), lambda b, c: (c, 0, 0)),
            pl.BlockSpec((1, 3, 3 * tc), lambda b, c: (c, 0, 0)),
            pl.BlockSpec((1, 1, 3 * tc), lambda b, c: (c, 0, 0)),
        ],
        out_specs=(pl.BlockSpec((1, L, tc), lambda b, c: (b, 0, c)), pl.BlockSpec((1, L, tc), lambda b, c: (b, 0, c))),
        scratch_shapes=[pltpu.VMEM((L + 2 * CONV_PAD, D), BF16),
                        pltpu.VMEM((L // R, R + 2 * CONV_PAD, 3 * tc), F32)],
        compiler_params=_cparams(("parallel", "arbitrary"), VMEM_BIG),
        name="hyena_in_shortconv",
    )(x, mod, gain, winr, binr, wshr, bshr)

    tcl = min(LCONV_TC, D)
    tfq = min(LCONV_TF, L)
    y = pl.pallas_call(
        functools.partial(_hy_conv_kernel, L=L),
        out_shape=jax.ShapeDtypeStruct((B, L, D), BF16),
        grid=(B, D // tcl, L // tfq),
        in_specs=[
            pl.BlockSpec((1, L, tcl), lambda b, c, f: (b, 0, c)),
            pl.BlockSpec((1, L, tcl), lambda b, c, f: (b, 0, c)),
            pl.BlockSpec((tfq, L), lambda b, c, f: (f, 0)),
            pl.BlockSpec((tfq, L), lambda b, c, f: (f, 0)),
            pl.BlockSpec((L, tfq), lambda b, c, f: (0, f)),
            pl.BlockSpec((L, tfq), lambda b, c, f: (0, f)),
            pl.BlockSpec((tfq, tcl), lambda b, c, f: (f, c)),
            pl.BlockSpec((tfq, tcl), lambda b, c, f: (f, c)),
            pl.BlockSpec((1, tcl), lambda b, c, f: (0, c)),
            pl.BlockSpec((1, tcl), lambda b, c, f: (0, c)),
        ],
        out_specs=pl.BlockSpec((1, L, tcl), lambda b, c, f: (b, 0, c)),
        scratch_shapes=[pltpu.VMEM((L, tcl), BF16), pltpu.VMEM((L, tcl), F32)],
        compiler_params=_cparams(("parallel", "parallel", "arbitrary"), VMEM_BIG),
        name="hyena_longconv",
    )(vx, x0, cos_t, sin_t, cos_t, sin_t, kc, ks, kny, skip)
    return y, w_out, b_out, None


def kernel(x, c, ctx, c_ctx, ada_w, ada_b, norm_mix, norm_ffn, mla_w_dq, mla_q_norm, mla_w_uq, mla_w_dkv, mla_kv_norm, mla_w_ukv, mla_qk_gain, mla_w_o, cf_w_pw1, cf_b_pw1, cf_w_dw, cf_b_dw, cf_ln_g, cf_ln_b, cf_w_pw2, cf_b_pw2, hy_w_in, hy_b_in, hy_w_short, hy_b_short, hy_f_w1, hy_f_b1, hy_f_w2, hy_f_b2, hy_f_w3, hy_sin_freq, hy_skip, hy_w_out, hy_b_out, ffn_w_up, ffn_w_dw, ffn_b_dw, ffn_w_down):
    B, L, D = x.shape
    Lc = ctx.shape[1]
    depth = ada_w.shape[0]

    rows = ((B + 1 + 7) // 8) * 8
    cvec = jnp.concatenate([c, c_ctx[None, :], jnp.zeros((rows - B - 1, D), F32)], axis=0)
    ada = _ada_all(cvec, ada_w, ada_b)
    zero_bias = jnp.zeros((1, D), F32)
    rope = None
    fw = _prep_ffn(ffn_w_up, ffn_w_dw, ffn_b_dw, ffn_w_down)

    for i in range(depth):
        kind = i % N_MIXERS
        j = i // N_MIXERS
        need_ctx_out = i < depth - 1
        modl = ada[i, :B].reshape(B, 6, D)
        modc = ada[i, B:B + 1].reshape(1, 6, D)
        gmix = norm_mix[i].reshape(1, D)
        gffn = norm_ffn[i].reshape(1, D)
        if kind == 0:
            mw = _prep_mla(mla_w_dq[j], mla_q_norm[j], mla_w_uq[j], mla_w_dkv[j], mla_kv_norm[j], mla_w_ukv[j],
                           mla_qk_gain[j], mla_w_o[j])
            if rope is None:
                rope = _rope_table(Lc, L)
            q, k, vt = _mla_prep(ctx, x, modl, modc, gmix, mw, rope)
            mix_l = (_attention(q, k, vt, 0, L, min(ATTN_TQ, L), 0, Lc + L), mw[-1], zero_bias, None)
            if need_ctx_out:
                mix_c = (_attention(q, k, vt, L // Lc, Lc, Lc, L // Lc, Lc, N_HEADS), mw[-1], zero_bias, None)
        elif kind == 1:
            cw = _prep_conf(cf_w_pw1[j], cf_b_pw1[j], cf_w_dw[j], cf_b_dw[j], cf_ln_g[j], cf_ln_b[j], cf_w_pw2[j],
                            cf_b_pw2[j])
            mix_l = _conformer(x, modl, gmix, cw)
            if need_ctx_out:
                mix_c = _conformer(ctx, modc, gmix, cw)
        else:
            hw = _prep_hyena(hy_w_in[j], hy_b_in[j], hy_w_short[j], hy_b_short[j], hy_f_w1[j], hy_f_b1[j],
                             hy_f_w2[j], hy_f_b2[j], hy_f_w3[j], hy_sin_freq[j], hy_skip[j], hy_w_out[j],
                             hy_b_out[j])
            mix_l = _hyena(x, modl, gmix, hw, _hyena_filter(L, hw))
            if need_ctx_out:
                mix_c = _hyena(ctx, modc, gmix, hw, _hyena_filter(Lc, hw))
        x = _ffn(x, modl, gffn, fw, i, mix_l)
        if need_ctx_out:
            ctx = _ffn(ctx, modc, gffn, fw, i, mix_c)
    return x
```
